```python
import math
import jax
import jax.numpy as jnp
from jax import lax
import numpy as np

D_MODEL = 2048
BATCH = 2
SEQ = 8192
DEPTH = 4
DEC_BATCH = 2
DEC_SEQ = 4096
PAST_LEN = 128

N_MIXERS = 3
N_LAYERS_A = (DEPTH + 2) // N_MIXERS
N_LAYERS_B = (DEPTH + 1) // N_MIXERS
N_LAYERS_C = DEPTH // N_MIXERS

A_WINDOWS = (128, 512, 2048)
A_DILATIONS = (1, 4, 16)
A_GROUPS = len(A_WINDOWS)
A_HEADS = 16
A_HEAD_DIM = D_MODEL // A_HEADS
ROPE_THETA = 10000.0

S5_CH = 16
S5_GROUPS = D_MODEL // S5_CH
S5_STATE = 64
S5_MAX_RE = -1e-4

C_HEADS = 16
C_KDIM = D_MODEL // C_HEADS
C_VDIM = D_MODEL // C_HEADS
C_CHUNK = 32

N_EXPERTS = 32
N_EXPERT_GROUPS = 8
EXPERTS_PER_GROUP = N_EXPERTS // N_EXPERT_GROUPS
TOP_K = 2
D_FF_EXPERT = 1024
MOE_BLOCK = 128

DN_ALPHA = (2 * DEPTH) ** 0.25
DN_BETA = (8 * DEPTH) ** -0.25
LN_EPS = 1e-5
RMS_EPS = 1e-6

kernel_name = 'hybrid_dilated_s5_hgrn2_moe_encoder'


def layer_norm(x, g, b):
    xf = x.astype(jnp.float32)
    xc = xf - jnp.mean(xf, -1, keepdims=True)
    var = jnp.mean(xc * xc, -1, keepdims=True)
    return (xc * lax.rsqrt(var + LN_EPS) * g.astype(jnp.float32) + b.astype(jnp.float32)).astype(x.dtype)


def rotary(t, pos):
    half = A_HEAD_DIM // 2
    inv_freq = ROPE_THETA ** (-jnp.arange(half, dtype=jnp.float32) / half)
    ang = pos.astype(jnp.float32)[:, None] * inv_freq[None, :]
    cos = jnp.cos(ang)[None, :, None, :]
    sin = jnp.sin(ang)[None, :, None, :]
    tf = t.astype(jnp.float32)
    t1, t2 = tf[..., :half], tf[..., half:]
    return jnp.concatenate([t1 * cos - t2 * sin, t2 * cos + t1 * sin], -1).astype(t.dtype)


def dilated_window_attn(q, k, v, dil, half):
    B_, S, H, Dh = q.shape
    L = S // dil
    nb = -(-L // half)
    Lp = nb * half

    def to_sub(t):
        t = t.reshape(B_, L, dil, H, Dh).transpose(0, 2, 1, 3, 4).reshape(B_ * dil, L, H, Dh)
        return jnp.pad(t, ((0, 0), (0, Lp - L), (0, 0), (0, 0))).reshape(B_ * dil, nb, half, H, Dh)

    def to_band(t):
        t = jnp.pad(t, ((0, 0), (1, 1), (0, 0), (0, 0), (0, 0)))
        return jnp.concatenate([t[:, :-2], t[:, 1:-1], t[:, 2:]], axis=2)

    qb = to_sub(q)
    kb = to_band(to_sub(k))
    vb = to_band(to_sub(v))
    s = jnp.einsum('nbqhd,nbkhd->nbhqk', qb, kb).astype(jnp.float32) * (Dh ** -0.5)
    qpos = (jnp.arange(nb) * half)[:, None] + jnp.arange(half)[None, :]
    kpos = (jnp.arange(nb) * half - half)[:, None] + jnp.arange(3 * half)[None, :]
    rel = kpos[:, None, :] - qpos[:, :, None]
    valid = (jnp.abs(rel) <= half) & (kpos[:, None, :] >= 0) & (kpos[:, None, :] < L)
    s = jnp.where(valid[None, :, None], s, -jnp.inf)
    m = jnp.max(s, -1, keepdims=True)
    p = jnp.exp(s - m)
    den = jnp.sum(p, -1)
    o = jnp.einsum('nbhqk,nbkhd->nbqhd', p, vb.astype(jnp.float32)) / jnp.swapaxes(den, 2, 3)[..., None]
    lse = jnp.swapaxes(m[..., 0] + jnp.log(den), 2, 3)

    def from_sub(t):
        t = t.reshape((B_, dil, Lp) + t.shape[3:])[:, :, :L]
        t = jnp.moveaxis(t, 1, 2)
        return t.reshape((B_, S) + t.shape[3:])

    return from_sub(o), from_sub(lse)


def dilated_attention_mixer(x, w_qkv, w_o):
    B_, S, _ = x.shape
    qkv = (x @ w_qkv).reshape(B_, S, 3, A_GROUPS, A_HEADS, A_HEAD_DIM)
    pos = jnp.arange(S)
    outs, lses = [], []
    for g in range(A_GROUPS):
        dil = A_DILATIONS[g]
        half = A_WINDOWS[g] // (2 * dil)
        q = rotary(qkv[:, :, 0, g], pos)
        k = rotary(qkv[:, :, 1, g], pos)
        o, lse = dilated_window_attn(q, k, qkv[:, :, 2, g], dil, half)
        outs.append(o)
        lses.append(lse)
    wts = jax.nn.softmax(jnp.stack(lses, 0), axis=0)
    o = jnp.einsum('gbsh,gbshd->bshd', wts, jnp.stack(outs, 0))
    return o.reshape(B_, S, A_HEADS * A_HEAD_DIM).astype(x.dtype) @ w_o


def _linear_combine(e1, e2):
    a1, b1 = e1
    a2, b2 = e2
    return a1 * a2, a2 * b1 + b2


def s5_mixer(x, w_in, lam_re, lam_im, log_dt, b_re, b_im, c_re, c_im, d_skip, w_glu):
    B_, S, D = x.shape
    u = (x @ w_in).astype(jnp.float32).reshape(B_, S, S5_GROUPS, S5_CH)
    uc = u.astype(jnp.complex64)
    y = u * d_skip.astype(jnp.float32).reshape(S5_GROUPS, S5_CH)
    for r in range(2):
        lam = lax.complex(jnp.minimum(lam_re[r].astype(jnp.float32), S5_MAX_RE), lam_im[r].astype(jnp.float32))
        dt = jnp.exp(log_dt[r].astype(jnp.float32))[:, None]
        lam_bar = jnp.exp(lam * dt)
        b_mat = lax.complex(b_re[r].astype(jnp.float32), b_im[r].astype(jnp.float32))
        b_bar = ((lam_bar - 1.0) / lam)[..., None] * b_mat
        bu = jnp.einsum('bsgc,gpc->bsgp', uc, b_bar)
        a = jnp.broadcast_to(lam_bar, (1, S) + lam_bar.shape)
        _, h = lax.associative_scan(_linear_combine, (a, bu), reverse=(r == 1), axis=1)
        c_mat = lax.complex(c_re[r].astype(jnp.float32), c_im[r].astype(jnp.float32))
        y = y + jnp.real(jnp.einsum('bsgp,gcp->bsgc', h, c_mat))
    y = jax.nn.gelu(y.reshape(B_, S, D)).astype(x.dtype)
    val, gate = jnp.split(y @ w_glu, 2, axis=-1)
    return val * jax.nn.sigmoid(gate)


def hgrn2_scan(q, k, v, logf):
    B_, S, H, K = q.shape
    V = v.shape[-1]
    n = S // C_CHUNK

    def chunks(t):
        return t.reshape(B_, n, C_CHUNK, H, t.shape[-1]).transpose(1, 0, 3, 2, 4)

    qc, kc, vc, gc = chunks(q), chunks(k), chunks(v), chunks(logf)
    bcum = jnp.cumsum(gc, axis=3)
    b_last = bcum[:, :, :, -1]
    q_dec = qc * jnp.exp(bcum)
    k_inv = kc * jnp.exp(-bcum)
    k_end = kc * jnp.exp(b_last[:, :, :, None, :] - bcum)
    lower_tri = jnp.tril(jnp.ones((C_CHUNK, C_CHUNK), dtype=bool))
    attn = jnp.where(lower_tri, jnp.einsum('nbhtk,nbhsk->nbhts', q_dec, k_inv), 0.0)
    intra = jnp.einsum('nbhts,nbhsv->nbhtv', attn, vc)

    def step(state, inp):
        qd, ke, vv, bl = inp
        out = jnp.einsum('bhtk,bhkv->bhtv', qd, state)
        state = state * jnp.exp(bl)[..., None] + jnp.einsum('bhtk,bhtv->bhkv', ke, vv)
        return state, out

    state0 = jnp.zeros((B_, H, K, V), jnp.float32)
    _, inter = lax.scan(step, state0, (q_dec, k_end, vc, b_last))
    o = intra + inter
    return o.transpose(1, 0, 3, 2, 4).reshape(B_, S, H, V)


def hgrn2_mixer(x, w_in, lb, norm_g, w_o):
    B_, S, D = x.shape
    zq, zf_fwd, zf_bwd, zi, zg = jnp.split(x @ w_in, 5, axis=-1)

    def heads(t):
        return t.astype(jnp.float32).reshape(B_, S, C_HEADS, -1)

    q = jax.nn.silu(heads(zq))
    i_val = heads(zi)
    lbh = lb.astype(jnp.float32).reshape(C_HEADS, C_KDIM)
    outs = []
    for zf, rev in ((zf_fwd, False), (zf_bwd, True)):
        f = lbh + (1.0 - lbh) * jax.nn.sigmoid(heads(zf))
        args = (q, 1.0 - f, i_val, jnp.log(f))
        if rev:
            args = tuple(jnp.flip(t, 1) for t in args)
        od = hgrn2_scan(*args)
        outs.append(jnp.flip(od, 1) if rev else od)
    o = outs[0] + outs[1]
    o = o * lax.rsqrt(jnp.mean(o * o, -1, keepdims=True) + RMS_EPS) * norm_g.astype(jnp.float32)
    o = o * jax.nn.silu(heads(zg))
    return o.reshape(B_, S, D).astype(x.dtype) @ w_o


def moe_layer(x, router_w, router_bias, w1, w3, w2):
    B_, S, D = x.shape
    T = B_ * S
    xt = x.reshape(T, D)
    scores = jax.nn.sigmoid((xt @ router_w).astype(jnp.float32))
    sel = (scores + router_bias.astype(jnp.float32)).reshape(T, N_EXPERT_GROUPS, EXPERTS_PER_GROUP)
    group_score = jnp.sum(lax.top_k(sel, TOP_K)[0], axis=-1)
    group_mask = jnp.argmax(group_score, -1)[:, None] == jnp.arange(N_EXPERT_GROUPS)[None, :]
    sel = jnp.where(group_mask[:, :, None], sel, -jnp.inf).reshape(T, N_EXPERTS)
    _, eidx = lax.top_k(sel, TOP_K)
    gates = jnp.take_along_axis(scores, eidx, axis=1)
    gates = gates / jnp.sum(gates, -1, keepdims=True)
    A = T * TOP_K
    flat_e = eidx.reshape(A)
    order = jnp.argsort(flat_e)
    se = flat_e[order]
    tok = order // TOP_K
    counts = jnp.bincount(flat_e, length=N_EXPERTS)
    padded = (counts + MOE_BLOCK - 1) // MOE_BLOCK * MOE_BLOCK
    pend = jnp.cumsum(padded)
    pstart = pend - padded
    start = jnp.cumsum(counts) - counts
    dest = pstart[se] + jnp.arange(A) - start[se]
    n_blocks = -(-A // MOE_BLOCK) + N_EXPERTS
    rows = n_blocks * MOE_BLOCK
    row_tok = jnp.zeros((rows,), jnp.int32).at[dest].set(tok.astype(jnp.int32))
    xr = xt[row_tok].reshape(n_blocks, MOE_BLOCK, D)
    blk_e = jnp.minimum(jnp.searchsorted(pend, jnp.arange(n_blocks) * MOE_BLOCK, side='right'), N_EXPERTS - 1)

    def expert_block(args):
        xb, e = args
        return (jax.nn.silu(xb @ w1[e]) * (xb @ w3[e])) @ w2[e]

    yr = lax.map(expert_block, (xr, blk_e)).reshape(rows, D)
    ya = yr[dest].astype(jnp.float32) * gates.reshape(A)[order][:, None]
    return jnp.zeros((T, D), jnp.float32).at[tok].add(ya).astype(x.dtype).reshape(B_, S, D)


def encoder_trunk(x, p):
    sm = jax.nn.softmax(p['c_lower_bounds'].astype(jnp.float32), axis=0)
    lower_bounds = jnp.cumsum(sm, axis=0) - sm[0]
    for i in range(DEPTH):
        j = i // N_MIXERS
        kind = i % N_MIXERS
        if kind == 0:
            h = dilated_attention_mixer(x, p['a_w_qkv'][j], p['a_w_o'][j])
        elif kind == 1:
            h = s5_mixer(x, p['b_w_in'][j], p['b_lam_re'][j], p['b_lam_im'][j], p['b_log_dt'][j],
                         p['b_b_re'][j], p['b_b_im'][j], p['b_c_re'][j], p['b_c_im'][j],
                         p['b_d'][j], p['b_w_glu'][j])
        else:
            h = hgrn2_mixer(x, p['c_w_in'][j], lower_bounds[i], p['c_norm_g'][j], p['c_w_o'][j])
        x = layer_norm(DN_ALPHA * x + h, p['ln1_g'][i], p['ln1_b'][i])
        f = moe_layer(x, p['router_w'], p['router_bias'], p['moe_w1'][i], p['moe_w3'][i], p['moe_w2'][i])
        x = layer_norm(DN_ALPHA * x + f, p['ln2_g'][i], p['ln2_b'][i])
    return x


def setup_inputs(seed: int = 0) -> dict:
    key = jax.random.key(seed)
    ks = iter(jax.random.split(key, 40))
    f32 = jnp.float32
    D = D_MODEL

    def nrm(shape, scale):
        return scale * jax.random.normal(next(ks), shape, f32)

    qkv_cols = 3 * A_GROUPS * A_HEADS * A_HEAD_DIM
    a_width = A_HEADS * A_HEAD_DIM
    lam_shape = (N_LAYERS_B, 2, S5_GROUPS, S5_STATE)
    bc_shape = (N_LAYERS_B, 2, S5_GROUPS, S5_STATE, S5_CH)
    cb_shape = (N_LAYERS_B, 2, S5_GROUPS, S5_CH, S5_STATE)
    return {
        'x_prompt': nrm((BATCH, SEQ, D), 1.0),
        'x_sample': nrm((DEC_BATCH, DEC_SEQ, D), 1.0),
        'a_w_qkv': nrm((N_LAYERS_A, D, qkv_cols), D ** -0.5),
        'a_w_o': nrm((N_LAYERS_A, a_width, D), a_width ** -0.5 * DN_BETA),
        'b_w_in': nrm((N_LAYERS_B, D, D), D ** -0.5),
        'b_lam_re': -0.5 + nrm(lam_shape, 0.01),
        'b_lam_im': math.pi * jnp.arange(S5_STATE, dtype=f32) + nrm(lam_shape, 0.01),
        'b_log_dt': jax.random.uniform(next(ks), (N_LAYERS_B, 2, S5_GROUPS), f32, math.log(1e-3), math.log(1e-1)),
        'b_b_re': nrm(bc_shape, (2 * S5_CH) ** -0.5),
        'b_b_im': nrm(bc_shape, (2 * S5_CH) ** -0.5),
        'b_c_re': nrm(cb_shape, (2 * S5_STATE) ** -0.5),
        'b_c_im': nrm(cb_shape, (2 * S5_STATE) ** -0.5),
        'b_d': nrm((N_LAYERS_B, D), 1.0),
        'b_w_glu': jnp.concatenate([nrm((N_LAYERS_B, D, D), D ** -0.5 * DN_BETA),
                                    nrm((N_LAYERS_B, D, D), D ** -0.5)], axis=-1),
        'c_w_in': nrm((N_LAYERS_C, D, 5 * D), D ** -0.5),
        'c_lower_bounds': nrm((DEPTH, C_HEADS * C_KDIM), 0.1),
        'c_norm_g': 1.0 + nrm((N_LAYERS_C, C_VDIM), 0.01),
        'c_w_o': nrm((N_LAYERS_C, D, D), D ** -0.5 * DN_BETA),
        'router_w': nrm((D, N_EXPERTS), D ** -0.5),
        'router_bias': nrm((N_EXPERTS,), 0.01),
        'moe_w1': nrm((DEPTH, N_EXPERTS, D, D_FF_EXPERT), D ** -0.5),
        'moe_w3': nrm((DEPTH, N_EXPERTS, D, D_FF_EXPERT), D ** -0.5),
        'moe_w2': nrm((DEPTH, N_EXPERTS, D_FF_EXPERT, D), D_FF_EXPERT ** -0.5 * DN_BETA),
        'ln1_g': 1.0 + nrm((DEPTH, D), 0.01),
        'ln1_b': nrm((DEPTH, D), 0.01),
        'ln2_g': 1.0 + nrm((DEPTH, D), 0.01),
        'ln2_b': nrm((DEPTH, D), 0.01),
    }


def reference(x_prompt, x_sample, a_w_qkv, a_w_o, b_w_in, b_lam_re, b_lam_im, b_log_dt, b_b_re, b_b_im,
              b_c_re, b_c_im, b_d, b_w_glu, c_w_in, c_lower_bounds, c_norm_g, c_w_o, router_w, router_bias,
              moe_w1, moe_w3, moe_w2, ln1_g, ln1_b, ln2_g, ln2_b):
    p = {
        'a_w_qkv': a_w_qkv, 'a_w_o': a_w_o,
        'b_w_in': b_w_in, 'b_lam_re': b_lam_re, 'b_lam_im': b_lam_im, 'b_log_dt': b_log_dt,
        'b_b_re': b_b_re, 'b_b_im': b_b_im, 'b_c_re': b_c_re, 'b_c_im': b_c_im, 'b_d': b_d, 'b_w_glu': b_w_glu,
        'c_w_in': c_w_in, 'c_lower_bounds': c_lower_bounds, 'c_norm_g': c_norm_g, 'c_w_o': c_w_o,
        'router_w': router_w, 'router_bias': router_bias,
        'moe_w1': moe_w1, 'moe_w3': moe_w3, 'moe_w2': moe_w2,
        'ln1_g': ln1_g, 'ln1_b': ln1_b, 'ln2_g': ln2_g, 'ln2_b': ln2_b,
    }
    y_prompt = encoder_trunk(x_prompt, p)
    y_sample = encoder_trunk(x_sample, p)
    return (y_prompt, y_sample)
```

```python
import functools
import math

import jax
import jax.numpy as jnp
import numpy as np
from jax import lax
from jax.experimental import pallas as pl
from jax.experimental.pallas import tpu as pltpu

F32 = jnp.float32
BF16 = jnp.bfloat16

LANES = 128
SUBLANES = 8
VMEM_LIMIT = 56 * 1024 * 1024

A_WINDOWS = (128, 512, 2048)
A_DILATIONS = (1, 4, 16)
A_HEAD_DIM = 128
ROPE_THETA = 10000.0
S5_CH = 16
S5_STATE = 64
S5_MAX_RE = -1e-4
C_KDIM = 128
C_CHUNK = 32
N_EXPERTS = 32
N_EXPERT_GROUPS = 8
EXPERTS_PER_GROUP = N_EXPERTS // N_EXPERT_GROUPS
TOP_K = 2
LN_EPS = 1e-5
RMS_EPS = 1e-6
N_MIXERS = 3
NEG_BIG = -1e30


def _tile(n, pref, unit=LANES):
    if n <= pref:
        return n
    t = pref - pref % unit
    while n % t:
        t -= unit
    return t


def _cparams(n_axes):
    return pltpu.CompilerParams(dimension_semantics=("arbitrary",) * n_axes, vmem_limit_bytes=VMEM_LIMIT)


def _mm_body(x_ref, w_ref, o_ref):
    acc = jnp.dot(x_ref[...].astype(BF16), w_ref[...].astype(BF16), preferred_element_type=F32)
    o_ref[...] = acc.astype(o_ref.dtype)


def matmul(x, w, out_dtype, tm=1024, tn=1024):
    M, K = x.shape
    N = w.shape[1]
    tm, tn = _tile(M, tm, SUBLANES), _tile(N, tn)
    return pl.pallas_call(
        _mm_body,
        grid=(M // tm, N // tn),
        in_specs=[pl.BlockSpec((tm, K), lambda i, j: (i, 0)), pl.BlockSpec((K, tn), lambda i, j: (0, j))],
        out_specs=pl.BlockSpec((tm, tn), lambda i, j: (i, j)),
        out_shape=jax.ShapeDtypeStruct((M, N), out_dtype),
        compiler_params=_cparams(2),
        name="matmul",
    )(x, w)


def _mm_rope_body(x_ref, w_ref, cos_ref, sin_ref, o_ref):
    acc = jnp.dot(x_ref[...].astype(BF16), w_ref[...].astype(BF16), preferred_element_type=F32)
    cos = cos_ref[...]
    sin = sin_ref[...]
    for h in range(acc.shape[1] // A_HEAD_DIM):
        sl = slice(h * A_HEAD_DIM, (h + 1) * A_HEAD_DIM)
        t = acc[:, sl]
        o_ref[:, sl] = (t * cos + pltpu.roll(t, A_HEAD_DIM // 2, 1) * sin).astype(o_ref.dtype)


def matmul_rope(x, w, cos, sin, out_dtype, tm=1024, tn=1024):
    M, K = x.shape
    N = w.shape[1]
    tm, tn = _tile(M, tm, SUBLANES), _tile(N, tn)
    return pl.pallas_call(
        _mm_rope_body,
        grid=(M // tm, N // tn),
        in_specs=[
            pl.BlockSpec((tm, K), lambda i, j: (i, 0)),
            pl.BlockSpec((K, tn), lambda i, j: (0, j)),
            pl.BlockSpec((tm, A_HEAD_DIM), lambda i, j: (i, 0)),
            pl.BlockSpec((tm, A_HEAD_DIM), lambda i, j: (i, 0)),
        ],
        out_specs=pl.BlockSpec((tm, tn), lambda i, j: (i, j)),
        out_shape=jax.ShapeDtypeStruct((M, N), out_dtype),
        compiler_params=_cparams(2),
        name="matmul_rope",
    )(x, w, cos, sin)


def _mm_glu_body(x_ref, wv_ref, wg_ref, o_ref):
    x = x_ref[...].astype(BF16)
    val = jnp.dot(x, wv_ref[...].astype(BF16), preferred_element_type=F32)
    gate = jnp.dot(x, wg_ref[...].astype(BF16), preferred_element_type=F32)
    o_ref[...] = (val * jax.nn.sigmoid(gate)).astype(o_ref.dtype)


def matmul_glu(x, w, out_dtype, tm=1024, tn=512):
    M, K = x.shape
    N = w.shape[1] // 2
    tm, tn = _tile(M, tm, SUBLANES), _tile(N, tn)
    nj = N // tn
    return pl.pallas_call(
        _mm_glu_body,
        grid=(M // tm, nj),
        in_specs=[
            pl.BlockSpec((tm, K), lambda i, j: (i, 0)),
            pl.BlockSpec((K, tn), lambda i, j: (0, j)),
            pl.BlockSpec((K, tn), lambda i, j: (0, j + nj)),
        ],
        out_specs=pl.BlockSpec((tm, tn), lambda i, j: (i, j)),
        out_shape=jax.ShapeDtypeStruct((M, N), out_dtype),
        compiler_params=_cparams(2),
        name="matmul_glu",
    )(x, w, w)


def _layer_norm_rows(y, g, b):
    yc = y - jnp.mean(y, -1, keepdims=True)
    var = jnp.mean(yc * yc, -1, keepdims=True)
    return yc * lax.rsqrt(var + LN_EPS) * g + b


def _res_ln_body(x_ref, h_ref, g_ref, b_ref, o_ref, ob_ref, *, alpha):
    y = alpha * x_ref[...] + h_ref[...].astype(F32)
    out = _layer_norm_rows(y, g_ref[...], b_ref[...])
    o_ref[...] = out
    ob_ref[...] = out.astype(BF16)


def residual_layer_norm(x, h, g, b, alpha, tm=512):
    M, D = x.shape
    tm = _tile(M, tm, SUBLANES)
    row = pl.BlockSpec((tm, D), lambda i: (i, 0))
    vec = pl.BlockSpec((1, D), lambda i: (0, 0))
    return pl.pallas_call(
        functools.partial(_res_ln_body, alpha=alpha),
        grid=(M // tm,),
        in_specs=[row, row, vec, vec],
        out_specs=[row, row],
        out_shape=[jax.ShapeDtypeStruct((M, D), F32), jax.ShapeDtypeStruct((M, D), BF16)],
        compiler_params=_cparams(1),
        name="residual_layer_norm",
    )(x, h, g.reshape(1, D).astype(F32), b.reshape(1, D).astype(F32))


def _segment_bounds(seq_lens, dil, tile):
    lo, hi = [], []
    start = 0
    for s in seq_lens:
        n = s // dil
        assert s % dil == 0 and n % tile == 0, (s, dil, tile)
        lo += [start] * (n // tile)
        hi += [start + n] * (n // tile)
        start += n
    return np.asarray(lo, np.int32), np.asarray(hi, np.int32)


def _attn_body(lo_ref, hi_ref, q_ref, k_ref, kp_ref, kn_ref, v_ref, vp_ref, vn_ref, o_ref, lse_ref,
               kbuf, vbuf, *, tq, nh, half, scale):
    i = pl.program_id(1)
    lo = lo_ref[i]
    hi = hi_ref[i]
    sb = 2 * half
    kbuf[0:half] = kp_ref[...]
    kbuf[half:half + tq] = k_ref[...]
    kbuf[half + tq:] = kn_ref[...]
    vbuf[0:half] = vp_ref[...]
    vbuf[half:half + tq] = v_ref[...]
    vbuf[half + tq:] = vn_ref[...]
    r = lax.broadcasted_iota(jnp.int32, (sb, 2 * sb), 0)
    c = lax.broadcasted_iota(jnp.int32, (sb, 2 * sb), 1)
    band = (c >= r) & (c <= r + 2 * half)
    lane = lax.broadcasted_iota(jnp.int32, (sb, LANES), 1)
    for j in range(tq // sb):
        kpos = i * tq + (j * sb - half) + c
        valid = band & (kpos >= lo) & (kpos < hi)
        lse_tile = jnp.zeros((sb, LANES), F32)
        for h in range(nh):
            hs = slice(h * A_HEAD_DIM, (h + 1) * A_HEAD_DIM)
            qj = q_ref[j * sb:(j + 1) * sb, hs]
            kj = kbuf[j * sb:(j + 2) * sb, hs]
            vj = vbuf[j * sb:(j + 2) * sb, hs]
            s = lax.dot_general(qj, kj, (((1,), (1,)), ((), ())), preferred_element_type=F32) * scale
            s = jnp.where(valid, s, NEG_BIG)
            m = jnp.max(s, -1, keepdims=True)
            p = jnp.exp(s - m)
            den = jnp.sum(p, -1, keepdims=True)
            o = jnp.dot(p.astype(BF16), vj, preferred_element_type=F32) / den
            o_ref[j * sb:(j + 1) * sb, hs] = o
            lse_tile = jnp.where(lane == h, m + jnp.log(den), lse_tile)
        lse_ref[j * sb:(j + 1) * sb, :] = lse_tile


def dilated_attention_group(qk, v, g, n_groups, seq_lens, tq=256):
    T = qk.shape[0]
    W = v.shape[1] // n_groups
    nh = W // A_HEAD_DIM
    dil = A_DILATIONS[g]
    half = A_WINDOWS[g] // (2 * dil)
    tq = min(tq, min(seq_lens) // max(A_DILATIONS))
    assert tq % (2 * half) == 0 and nh <= LANES
    rows = T // dil
    n_tiles = rows // tq
    hb = tq // half
    n_hblk = rows // half
    lo, hi = _segment_bounds(seq_lens, dil, tq)
    lo, hi = np.tile(lo, 1), np.tile(hi, 1)
    qk_v = qk.reshape(rows, dil * 2 * n_groups * W)
    v_v = v.reshape(rows, dil * n_groups * W)
    qcol = lambda r: r * 2 * n_groups + g
    kcol = lambda r: r * 2 * n_groups + n_groups + g
    vcol = lambda r: r * n_groups + g
    prev = lambda i: jnp.maximum(i * hb - 1, 0)
    nxt = lambda i: jnp.minimum((i + 1) * hb, n_hblk - 1)
    in_specs = [
        pl.BlockSpec((tq, W), lambda r, i, lo, hi: (i, qcol(r))),
        pl.BlockSpec((tq, W), lambda r, i, lo, hi: (i, kcol(r))),
        pl.BlockSpec((half, W), lambda r, i, lo, hi: (prev(i), kcol(r))),
        pl.BlockSpec((half, W), lambda r, i, lo, hi: (nxt(i), kcol(r))),
        pl.BlockSpec((tq, W), lambda r, i, lo, hi: (i, vcol(r))),
        pl.BlockSpec((half, W), lambda r, i, lo, hi: (prev(i), vcol(r))),
        pl.BlockSpec((half, W), lambda r, i, lo, hi: (nxt(i), vcol(r))),
    ]
    out_specs = [
        pl.BlockSpec((tq, W), lambda r, i, lo, hi: (i, r)),
        pl.BlockSpec((tq, LANES), lambda r, i, lo, hi: (i, r)),
    ]
    o, lse = pl.pallas_call(
        functools.partial(_attn_body, tq=tq, nh=nh, half=half, scale=A_HEAD_DIM ** -0.5),
        grid_spec=pltpu.PrefetchScalarGridSpec(
            num_scalar_prefetch=2,
            grid=(dil, n_tiles),
            in_specs=in_specs,
            out_specs=out_specs,
            scratch_shapes=[pltpu.VMEM((tq + 2 * half, W), BF16), pltpu.VMEM((tq + 2 * half, W), BF16)],
        ),
        out_shape=[jax.ShapeDtypeStruct((rows, dil * W), F32), jax.ShapeDtypeStruct((rows, dil * LANES), F32)],
        compiler_params=_cparams(2),
        name=f"dilated_attention_g{g}",
    )(jnp.asarray(lo), jnp.asarray(hi), qk_v, qk_v, qk_v, qk_v, v_v, v_v, v_v)
    return o.reshape(T, W), lse.reshape(T, LANES)


def _attn_combine_body(*refs, n_groups, nh):
    o_refs = refs[:n_groups]
    l_refs = refs[n_groups:2 * n_groups]
    out_ref = refs[2 * n_groups]
    ls = [l[...] for l in l_refs]
    m = functools.reduce(jnp.maximum, ls)
    es = [jnp.exp(l - m) for l in ls]
    den = functools.reduce(jnp.add, es)
    ws = [e / den for e in es]
    for h in range(nh):
        hs = slice(h * A_HEAD_DIM, (h + 1) * A_HEAD_DIM)
        acc = ws[0][:, h:h + 1] * o_refs[0][:, hs]
        for g in range(1, n_groups):
            acc = acc + ws[g][:, h:h + 1] * o_refs[g][:, hs]
        out_ref[:, hs] = acc.astype(out_ref.dtype)


def attention_combine(outs, lses, tm=512):
    n_groups = len(outs)
    T, W = outs[0].shape
    tm = _tile(T, tm, SUBLANES)
    row = pl.BlockSpec((tm, W), lambda i: (i, 0))
    lrow = pl.BlockSpec((tm, LANES), lambda i: (i, 0))
    return pl.pallas_call(
        functools.partial(_attn_combine_body, n_groups=n_groups, nh=W // A_HEAD_DIM),
        grid=(T // tm,),
        in_specs=[row] * n_groups + [lrow] * n_groups,
        out_specs=row,
        out_shape=jax.ShapeDtypeStruct((T, W), BF16),
        compiler_params=_cparams(1),
        name="attention_combine",
    )(*outs, *lses)


def _rope_tables(seq_lens):
    half = A_HEAD_DIM // 2
    inv_freq = ROPE_THETA ** (-jnp.arange(half, dtype=F32) / half)
    pos = jnp.concatenate([jnp.arange(s, dtype=F32) for s in seq_lens])
    ang = pos[:, None] * inv_freq[None, :]
    cos, sin = jnp.cos(ang), jnp.sin(ang)
    return jnp.concatenate([cos, cos], -1), jnp.concatenate([-sin, sin], -1)


def dilated_attention_mixer(xb, w_qkv, w_o, seq_lens):
    n_groups = len(A_WINDOWS)
    D = xb.shape[1]
    W = w_qkv.shape[1] // (3 * n_groups)
    wb = w_qkv.astype(BF16)
    cos, sin = _rope_tables(seq_lens)
    qk = matmul_rope(xb, wb[:, :2 * n_groups * W], cos, sin, BF16)
    v = matmul(xb, wb[:, 2 * n_groups * W:], BF16)
    outs, lses = [], []
    for g in range(n_groups):
        o, lse = dilated_attention_group(qk, v, g, n_groups, seq_lens)
        outs.append(o)
        lses.append(lse)
    o = attention_combine(outs, lses)
    return matmul(o, w_o.astype(BF16), F32)


S5_GROUP_BLOCK = LANES // S5_CH
S5_NS = S5_GROUP_BLOCK * S5_STATE
S5_STREAMS = SUBLANES


def _cmul(ar, ai, br, bi):
    return ar * br - ai * bi, ar * bi + ai * br


def _s5_body(reset_ref, u_ref, d_ref, bblk_ref, cblk_ref, lam_ref, *rest, tt, rev, final):
    if final:
        yin_ref, o_ref, bu_ref, carry_ref = rest
    else:
        o_ref, bu_ref, carry_ref = rest
    i = pl.program_id(1)
    ns = S5_NS
    lc = tt // S5_STREAMS

    @pl.when(reset_ref[i] == 1)
    def _():
        carry_ref[...] = jnp.zeros_like(carry_ref)

    nc = ns // LANES
    u = u_ref[...]
    bu = jnp.dot(u.astype(BF16), bblk_ref[0], preferred_element_type=F32)
    for c in range(2 * nc):
        bu_ref[c] = bu[:, c * LANES:(c + 1) * LANES]
    lam = lam_ref[0]
    order = range(lc - 1, -1, -1) if rev else range(lc)
    strided = lambda j: pl.ds(j, S5_STREAMS, stride=lc)
    row = lax.broadcasted_iota(jnp.int32, (S5_STREAMS, LANES), 0)
    last = 0 if rev else S5_STREAMS - 1
    first = S5_STREAMS - 1 if rev else 0
    shift1 = S5_STREAMS - 1 if rev else 1

    for c in range(nc):
        cs = slice(c * LANES, (c + 1) * LANES)
        rows = lambda k: jnp.broadcast_to(lam[k:k + 1, cs], (S5_STREAMS, LANES))
        lr, li = rows(0), rows(1)
        bre, bim = bu_ref.at[c], bu_ref.at[nc + c]

        hr = jnp.zeros((S5_STREAMS, LANES), F32)
        hi = jnp.zeros((S5_STREAMS, LANES), F32)
        for j in order:
            pr, pi = _cmul(lr, li, hr, hi)
            hr, hi = pr + bre[strided(j), :], pi + bim[strided(j), :]
        er, ei = hr, hi

        cr = jnp.broadcast_to(carry_ref[last:last + 1, cs], (S5_STREAMS, LANES))
        ci = jnp.broadcast_to(carry_ref[last:last + 1, ns + c * LANES:ns + (c + 1) * LANES], (S5_STREAMS, LANES))
        sr = jnp.where(row == first, cr, pltpu.roll(er, shift1, 0))
        si = jnp.where(row == first, ci, pltpu.roll(ei, shift1, 0))
        for n, k in enumerate((1, 2, 4)):
            ar, ai = rows(2 + 2 * n), rows(3 + 2 * n)
            keep = (row <= S5_STREAMS - 1 - k) if rev else (row >= k)
            shift = S5_STREAMS - k if rev else k
            qr = jnp.where(keep, pltpu.roll(sr, shift, 0), 0.0)
            qi = jnp.where(keep, pltpu.roll(si, shift, 0), 0.0)
            pr, pi = _cmul(ar, ai, qr, qi)
            sr, si = sr + pr, si + pi
        pr, pi = _cmul(rows(2), rows(3), sr, si)
        carry_ref[:, cs] = pr + er
        carry_ref[:, ns + c * LANES:ns + (c + 1) * LANES] = pi + ei

        hr, hi = sr, si
        for j in order:
            pr, pi = _cmul(lr, li, hr, hi)
            hr, hi = pr + bre[strided(j), :], pi + bim[strided(j), :]
            bre[strided(j), :] = hr
            bim[strided(j), :] = hi

    y = jnp.zeros((tt, LANES), F32)
    for c in range(2 * nc):
        y = y + jnp.dot(bu_ref[c].astype(BF16), cblk_ref[0, c * LANES:(c + 1) * LANES, :],
                        preferred_element_type=F32)
    if final:
        o_ref[...] = jax.nn.gelu(yin_ref[...] + y).astype(o_ref.dtype)
    else:
        o_ref[...] = u * d_ref[...] + y


def _s5_direction_params(lam_re, lam_im, log_dt, b_re, b_im, c_re, c_im, lc):
    G = lam_re.shape[0]
    nb = G // S5_GROUP_BLOCK
    a = jnp.minimum(lam_re.astype(F32), S5_MAX_RE)
    b = lam_im.astype(F32)
    dt = jnp.exp(log_dt.astype(F32))[:, None]

    def lam_pow(n):
        mag = jnp.exp(n * a * dt)
        return mag * jnp.cos(n * b * dt), mag * jnp.sin(n * b * dt)

    lbr, lbi = lam_pow(1.0)
    nr, ni = lbr - 1.0, lbi
    den = a * a + b * b
    fr, fi = (nr * a + ni * b) / den, (ni * a - nr * b) / den
    bbr = fr[..., None] * b_re - fi[..., None] * b_im
    bbi = fr[..., None] * b_im + fi[..., None] * b_re
    eye = jnp.eye(S5_GROUP_BLOCK, dtype=F32)

    def in_block(t):
        t = t.reshape(nb, S5_GROUP_BLOCK, S5_STATE, S5_CH)
        return jnp.einsum('ngpc,gh->ngchp', t, eye).reshape(nb, LANES, S5_NS)

    def out_block(t):
        t = t.reshape(nb, S5_GROUP_BLOCK, S5_CH, S5_STATE)
        return jnp.einsum('ngcp,gh->nhpgc', t, eye).reshape(nb, S5_NS, LANES)

    bblk = jnp.concatenate([in_block(bbr), in_block(bbi)], -1).astype(BF16)
    cblk = jnp.concatenate([out_block(c_re.astype(F32)), out_block(-c_im.astype(F32))], 1).astype(BF16)
    vecs = []
    for n in (1.0, float(lc), 2.0 * lc, 4.0 * lc):
        vecs += list(lam_pow(n))
    lam_rows = jnp.stack([v.reshape(nb, S5_NS) for v in vecs], 1)
    return bblk, cblk, lam_rows


def _s5_scan(u, d_skip, params, y_in, seq_lens, rev, tt):
    T, D = u.shape
    nb = D // LANES
    nt = T // tt
    bblk, cblk, lam_rows = params
    starts = np.cumsum([0] + list(seq_lens))[:-1] // tt
    ends = np.cumsum(list(seq_lens)) // tt - 1
    reset = np.zeros((nt,), np.int32)
    reset[(nt - 1 - ends) if rev else starts] = 1
    tmap = (lambda i: nt - 1 - i) if rev else (lambda i: i)
    final = y_in is not None
    tile = pl.BlockSpec((tt, LANES), lambda b, i, rs: (tmap(i), b))
    in_specs = [
        tile,
        pl.BlockSpec((1, LANES), lambda b, i, rs: (0, b)),
        pl.BlockSpec((1, LANES, 2 * S5_NS), lambda b, i, rs: (b, 0, 0)),
        pl.BlockSpec((1, 2 * S5_NS, LANES), lambda b, i, rs: (b, 0, 0)),
        pl.BlockSpec((1, SUBLANES, S5_NS), lambda b, i, rs: (b, 0, 0)),
    ]
    args = [u, d_skip.reshape(1, D).astype(F32), bblk, cblk, lam_rows]
    if final:
        in_specs.append(tile)
        args.append(y_in)
    return pl.pallas_call(
        functools.partial(_s5_body, tt=tt, rev=rev, final=final),
        grid_spec=pltpu.PrefetchScalarGridSpec(
            num_scalar_prefetch=1,
            grid=(nb, nt),
            in_specs=in_specs,
            out_specs=tile,
            scratch_shapes=[pltpu.VMEM((2 * S5_NS // LANES, tt, LANES), F32),
                            pltpu.VMEM((S5_STREAMS, 2 * S5_NS), F32)],
        ),
        out_shape=jax.ShapeDtypeStruct((T, D), BF16 if final else F32),
        compiler_params=_cparams(2),
        name="s5_scan_rev" if rev else "s5_scan_fwd",
    )(jnp.asarray(reset), *args)


def s5_mixer(xb, w_in, lam_re, lam_im, log_dt, b_re, b_im, c_re, c_im, d_skip, w_glu, seq_lens, tt=512):
    tt = min(tt, min(seq_lens))
    assert all(s % tt == 0 for s in seq_lens) and tt % S5_STREAMS == 0
    u = matmul(xb, w_in.astype(BF16), F32)
    lc = tt // S5_STREAMS
    prm = [_s5_direction_params(lam_re[r], lam_im[r], log_dt[r], b_re[r], b_im[r], c_re[r], c_im[r], lc)
           for r in range(2)]
    y = _s5_scan(u, d_skip, prm[0], None, seq_lens, False, tt)
    y = _s5_scan(u, d_skip, prm[1], y, seq_lens, True, tt)
    return matmul_glu(y, w_glu.astype(BF16), F32)


def _split3_bf16(x):
    hi = x.astype(BF16)
    r1 = x - hi.astype(F32)
    mid = r1.astype(BF16)
    lo = (r1 - mid.astype(F32)).astype(BF16)
    return hi, mid, lo


def _hgrn_body(reset_ref, zq_ref, zf_ref, zi_ref, lb_ref, *rest, tt, nh, rev, final):
    if final:
        zg_ref, ofwd_ref, ng_ref, o_ref, st_ref = rest
    else:
        o_ref, st_ref = rest
    i = pl.program_id(1)
    C = C_CHUNK

    @pl.when(reset_ref[i] == 1)
    def _():
        st_ref[...] = jnp.zeros_like(st_ref)

    r = lax.broadcasted_iota(jnp.int32, (tt, tt), 0)
    c = lax.broadcasted_iota(jnp.int32, (tt, tt), 1)
    same = (r // C) == (c // C)
    cum_m = same & ((c >= r) if rev else (c <= r))
    sum_m = jnp.concatenate([cum_m, same], 0).astype(BF16)
    lb = lb_ref[...]
    f = lb + (1.0 - lb) * jax.nn.sigmoid(zf_ref[...])
    k = 1.0 - f
    sums = functools.reduce(jnp.add, [jnp.dot(sum_m, part, preferred_element_type=F32)
                                      for part in _split3_bf16(jnp.log(f))])
    bcum, tot = sums[:tt], sums[tt:]
    qd = (jax.nn.silu(zq_ref[...]) * jnp.exp(bcum)).astype(BF16)
    ki = (k * jnp.exp(-bcum)).astype(BF16)
    ke = (k * jnp.exp(tot - bcum)).astype(BF16)
    v = zi_ref[...].astype(BF16)
    dec = jnp.exp(tot)
    rr = lax.broadcasted_iota(jnp.int32, (C, C), 0)
    cc = lax.broadcasted_iota(jnp.int32, (C, C), 1)
    tri = (cc >= rr) if rev else (cc <= rr)
    nt_dims = (((1,), (1,)), ((), ()))
    chunks = range(tt // C - 1, -1, -1) if rev else range(tt // C)
    for n in chunks:
        rs = slice(n * C, (n + 1) * C)
        for h in range(nh):
            hs = slice(h * C_KDIM, (h + 1) * C_KDIM)
            qh, vh = qd[rs, hs], v[rs, hs]
            att = lax.dot_general(qh, ki[rs, hs], nt_dims, preferred_element_type=F32)
            att = jnp.where(tri, att, 0.0).astype(BF16)
            st = st_ref[h]
            o = (jnp.dot(att, vh, preferred_element_type=F32)
                 + lax.dot_general(qh, st.astype(BF16), nt_dims, preferred_element_type=F32))
            st_ref[h] = st * dec[n * C:n * C + 1, hs] + jnp.dot(vh.T, ke[rs, hs], preferred_element_type=F32)
            if final:
                o = o + ofwd_ref[rs, hs]
                o = o * lax.rsqrt(jnp.mean(o * o, -1, keepdims=True) + RMS_EPS) * ng_ref[...]
                o = o * jax.nn.silu(zg_ref[rs, hs])
            o_ref[rs, hs] = o.astype(o_ref.dtype)


def _hgrn_scan(z, lb, seq_lens, rev, tt, nh, o_fwd=None, norm_g=None):
    T = z.shape[0]
    D = z.shape[1] // 5
    W = nh * C_KDIM
    nhb = D // W
    nt = T // tt
    starts = np.cumsum([0] + list(seq_lens))[:-1] // tt
    ends = np.cumsum(list(seq_lens)) // tt - 1
    reset = np.zeros((nt,), np.int32)
    reset[(nt - 1 - ends) if rev else starts] = 1
    tmap = (lambda i: nt - 1 - i) if rev else (lambda i: i)
    final = o_fwd is not None
    zcol = lambda n: pl.BlockSpec((tt, W), lambda b, i, rs: (tmap(i), n * nhb + b))
    in_specs = [zcol(0), zcol(2 if rev else 1), zcol(3), pl.BlockSpec((1, W), lambda b, i, rs: (0, b))]
    args = [z, z, z, lb.reshape(1, D).astype(F32)]
    if final:
        in_specs += [zcol(4), pl.BlockSpec((tt, W), lambda b, i, rs: (tmap(i), b)),
                     pl.BlockSpec((1, C_KDIM), lambda b, i, rs: (0, 0))]
        args += [z, o_fwd, norm_g.reshape(1, C_KDIM).astype(F32)]
    return pl.pallas_call(
        functools.partial(_hgrn_body, tt=tt, nh=nh, rev=rev, final=final),
        grid_spec=pltpu.PrefetchScalarGridSpec(
            num_scalar_prefetch=1,
            grid=(nhb, nt),
            in_specs=in_specs,
            out_specs=pl.BlockSpec((tt, W), lambda b, i, rs: (tmap(i), b)),
            scratch_shapes=[pltpu.VMEM((nh, C_KDIM, C_KDIM), F32)],
        ),
        out_shape=jax.ShapeDtypeStruct((T, D), BF16 if final else F32),
        compiler_params=_cparams(2),
        name="hgrn2_scan_rev" if rev else "hgrn2_scan_fwd",
    )(jnp.asarray(reset), *args)


def hgrn2_mixer(xb, w_in, lb, norm_g, w_o, seq_lens, tt=128, nh=4):
    assert all(s % tt == 0 for s in seq_lens) and tt % C_CHUNK == 0
    z = matmul(xb, w_in.astype(BF16), F32)
    o_fwd = _hgrn_scan(z, lb, seq_lens, False, tt, nh)
    o = _hgrn_scan(z, lb, seq_lens, True, tt, nh, o_fwd, norm_g)
    return matmul(o, w_o.astype(BF16), F32)


MOE_ROWS = 256


def _token_slab(d):
    return (d // LANES, LANES)


def _router_body(x_ref, wt_ref, bias_ref, eidx_ref, gate_ref):
    x = x_ref[...]
    wt = wt_ref[...]
    xh = x.astype(BF16)
    xl = (x - xh.astype(F32)).astype(BF16)
    wh = wt.astype(BF16)
    wl = (wt - wh.astype(F32)).astype(BF16)
    nt_dims = (((1,), (1,)), ((), ()))
    dg = lambda a, b: lax.dot_general(a, b, nt_dims, preferred_element_type=F32)
    scores = jax.nn.sigmoid(dg(wh, xh) + dg(wh, xl) + dg(wl, xh))
    sel = scores + bias_ref[...]
    ng = N_EXPERT_GROUPS
    s = [sel[j * ng:(j + 1) * ng] for j in range(EXPERTS_PER_GROUP)]
    sc = [scores[j * ng:(j + 1) * ng] for j in range(EXPERTS_PER_GROUP)]
    hi1, lo1 = jnp.maximum(s[0], s[1]), jnp.minimum(s[0], s[1])
    hi2, lo2 = jnp.maximum(s[2], s[3]), jnp.minimum(s[2], s[3])
    gs = jnp.maximum(hi1, hi2) + jnp.maximum(jnp.minimum(hi1, hi2), jnp.maximum(lo1, lo2))
    gi = lax.broadcasted_iota(jnp.int32, gs.shape, 0)
    gidx = jnp.min(jnp.where(gs == jnp.max(gs, 0, keepdims=True), gi, ng), 0, keepdims=True)
    gm = gi == gidx
    val = [jnp.sum(jnp.where(gm, t, 0.0), 0, keepdims=True) for t in s]
    scv = [jnp.sum(jnp.where(gm, t, 0.0), 0, keepdims=True) for t in sc]
    best, bj = val[0], jnp.zeros_like(gidx)
    for j in range(1, EXPERTS_PER_GROUP):
        upd = val[j] > best
        best, bj = jnp.where(upd, val[j], best), jnp.where(upd, j, bj)
    sec, sj = jnp.full_like(best, -jnp.inf), jnp.zeros_like(gidx)
    for j in range(EXPERTS_PER_GROUP):
        upd = (bj != j) & (val[j] > sec)
        sec, sj = jnp.where(upd, val[j], sec), jnp.where(upd, j, sj)
    pick = lambda idx: functools.reduce(jnp.add, [jnp.where(idx == j, scv[j], 0.0)
                                                  for j in range(EXPERTS_PER_GROUP)])
    g1, g2 = pick(bj), pick(sj)
    eidx_ref[0:1, :] = gidx * EXPERTS_PER_GROUP + bj
    eidx_ref[1:2, :] = gidx * EXPERTS_PER_GROUP + sj
    gate_ref[0:1, :] = g1 / (g1 + g2)
    gate_ref[1:2, :] = g2 / (g1 + g2)


def moe_router(x, router_w, router_bias, tm=512):
    T, D = x.shape
    tm = _tile(T, tm)
    perm = np.arange(N_EXPERTS).reshape(N_EXPERT_GROUPS, EXPERTS_PER_GROUP).T.reshape(-1)
    wt = router_w.astype(F32).T[perm]
    bias = router_bias.astype(F32)[perm].reshape(N_EXPERTS, 1)
    return pl.pallas_call(
        _router_body,
        grid=(T // tm,),
        in_specs=[pl.BlockSpec((tm, D), lambda i: (i, 0)), pl.BlockSpec((N_EXPERTS, D), lambda i: (0, 0)),
                  pl.BlockSpec((N_EXPERTS, 1), lambda i: (0, 0))],
        out_specs=[pl.BlockSpec((TOP_K, tm), lambda i: (0, i)), pl.BlockSpec((TOP_K, tm), lambda i: (0, i))],
        out_shape=[jax.ShapeDtypeStruct((TOP_K, T), jnp.int32), jax.ShapeDtypeStruct((TOP_K, T), F32)],
        compiler_params=_cparams(1),
        name="moe_router",
    )(x, wt, bias)


def _dispatch_plan(eidx, blk):
    T = eidx.shape[1]
    A = T * TOP_K
    flat_e = eidx.T.reshape(A)
    onehot = (flat_e[:, None] == jnp.arange(N_EXPERTS, dtype=jnp.int32)[None, :]).astype(jnp.int32)
    csum = jnp.cumsum(onehot, axis=0)
    counts = csum[-1]
    padded = (counts + blk - 1) // blk * blk
    pend = jnp.cumsum(padded)
    pstart = pend - padded
    pos = jnp.sum(onehot * (csum - 1 + pstart[None, :]), axis=1).astype(jnp.int32)
    n_blocks = A // blk + N_EXPERTS
    blk_e = jnp.minimum(jnp.searchsorted(pend, jnp.arange(n_blocks, dtype=jnp.int32) * blk, side='right'),
                        N_EXPERTS - 1).astype(jnp.int32)
    n_used = (pend[-1] // blk).astype(jnp.int32).reshape(1)
    return pos, blk_e, n_used, n_blocks


def _dispatch_body(pos_ref, x_ref, xr_in_ref, xr_ref, sem):
    del xr_in_ref
    tm = x_ref.shape[0]

    def copy(t, a):
        return pltpu.make_async_copy(x_ref.at[t], xr_ref.at[pos_ref[0, a]], sem)

    def issue(t, carry):
        for k in range(TOP_K):
            copy(t, TOP_K * t + k).start()
        return carry

    def drain(t, carry):
        for k in range(TOP_K):
            copy(t, TOP_K * t + k).wait()
        return carry

    lax.fori_loop(0, tm, issue, 0)
    lax.fori_loop(0, tm, drain, 0)


def moe_dispatch(x3, pos, rows, tm=256):
    T = x3.shape[0]
    tm = _tile(T, tm, SUBLANES)
    pos3 = pos.reshape(T // tm, 1, TOP_K * tm)
    return pl.pallas_call(
        _dispatch_body,
        grid=(T // tm,),
        in_specs=[pl.BlockSpec((None, 1, TOP_K * tm), lambda i: (i, 0, 0), memory_space=pltpu.SMEM),
                  pl.BlockSpec((tm,) + x3.shape[1:], lambda i: (i, 0, 0)),
                  pl.BlockSpec(memory_space=pl.ANY)],
        out_specs=pl.BlockSpec(memory_space=pl.ANY),
        out_shape=jax.ShapeDtypeStruct((rows,) + x3.shape[1:], x3.dtype),
        scratch_shapes=[pltpu.SemaphoreType.DMA(())],
        input_output_aliases={2: 0},
        compiler_params=_cparams(1),
        name="moe_dispatch",
    )(pos3, x3, jnp.zeros((rows,) + x3.shape[1:], x3.dtype))


def _slab_to_rows(ref):
    return jnp.concatenate([ref[:, j, :] for j in range(ref.shape[1])], axis=1)


def _expert_body(be_ref, nu_ref, x_ref, w1_ref, w3_ref, w2_ref, o_ref):
    del be_ref

    @pl.when(pl.program_id(0) < nu_ref[0])
    def _():
        x = _slab_to_rows(x_ref).astype(BF16)
        h1 = jnp.dot(x, w1_ref[0], preferred_element_type=F32)
        h3 = jnp.dot(x, w3_ref[0], preferred_element_type=F32)
        y = jnp.dot((jax.nn.silu(h1) * h3).astype(BF16), w2_ref[0], preferred_element_type=F32)
        w = o_ref.shape[2]
        for j in range(o_ref.shape[1]):
            o_ref[:, j, :] = y[:, j * w:(j + 1) * w]


def moe_experts(xr3, blk_e, n_used, w1, w3, w2, blk):
    rows = xr3.shape[0]
    E, D, F = w1.shape
    slab = pl.BlockSpec((blk,) + xr3.shape[1:], lambda b, be, nu: (b, 0, 0))
    return pl.pallas_call(
        _expert_body,
        grid_spec=pltpu.PrefetchScalarGridSpec(
            num_scalar_prefetch=2,
            grid=(rows // blk,),
            in_specs=[slab,
                      pl.BlockSpec((1, D, F), lambda b, be, nu: (be[b], 0, 0)),
                      pl.BlockSpec((1, D, F), lambda b, be, nu: (be[b], 0, 0)),
                      pl.BlockSpec((1, F, D), lambda b, be, nu: (be[b], 0, 0))],
            out_specs=slab,
        ),
        out_shape=jax.ShapeDtypeStruct(xr3.shape, F32),
        compiler_params=_cparams(1),
        name="moe_experts",
    )(blk_e, n_used, xr3, w1, w3, w2)


def _combine_body(pos_ref, gate_ref, x_ref, g_ref, b_ref, yr_ref, o_ref, ob_ref, buf0, buf1, sem, *, alpha):
    tm = x_ref.shape[0]
    bufs = (buf0, buf1)

    def copy(t, k):
        return pltpu.make_async_copy(yr_ref.at[pos_ref[0, TOP_K * t + k]], bufs[k].at[t], sem)

    def issue(t, carry):
        for k in range(TOP_K):
            copy(t, k).start()
        return carry

    def drain(t, carry):
        for k in range(TOP_K):
            copy(t, k).wait()
        return carry

    lax.fori_loop(0, tm, issue, 0)
    lax.fori_loop(0, tm, drain, 0)
    gates = gate_ref[...]
    f = gates[:, 0:1] * _slab_to_rows(buf0) + gates[:, 1:2] * _slab_to_rows(buf1)
    out = _layer_norm_rows(alpha * x_ref[...] + f, g_ref[...], b_ref[...])
    o_ref[...] = out
    ob_ref[...] = out.astype(BF16)


def moe_combine(x, yr3, pos, gates, g, b, alpha, tm=256):
    T, D = x.shape
    tm = _tile(T, tm, SUBLANES)
    pos3 = pos.reshape(T // tm, 1, TOP_K * tm)
    row = pl.BlockSpec((tm, D), lambda i: (i, 0))
    vec = pl.BlockSpec((1, D), lambda i: (0, 0))
    return pl.pallas_call(
        functools.partial(_combine_body, alpha=alpha),
        grid=(T // tm,),
        in_specs=[pl.BlockSpec((None, 1, TOP_K * tm), lambda i: (i, 0, 0), memory_space=pltpu.SMEM),
                  pl.BlockSpec((tm, TOP_K), lambda i: (i, 0)), row, vec, vec,
                  pl.BlockSpec(memory_space=pl.ANY)],
        out_specs=[row, row],
        out_shape=[jax.ShapeDtypeStruct((T, D), F32), jax.ShapeDtypeStruct((T, D), BF16)],
        scratch_shapes=[pltpu.VMEM((tm,) + yr3.shape[1:], F32), pltpu.VMEM((tm,) + yr3.shape[1:], F32),
                        pltpu.SemaphoreType.DMA(())],
        compiler_params=_cparams(1),
        name="moe_combine",
    )(pos3, gates, x, g.reshape(1, D).astype(F32), b.reshape(1, D).astype(F32), yr3)


def moe_block(x, xb, router_w, router_bias, w1, w3, w2, ln_g, ln_b, alpha):
    del xb
    T, D = x.shape
    eidx, gates = moe_router(x, router_w, router_bias)
    pos, blk_e, n_used, n_blocks = _dispatch_plan(eidx, MOE_ROWS)
    xr3 = moe_dispatch(x.reshape((T,) + _token_slab(D)), pos, n_blocks * MOE_ROWS)
    yr3 = moe_experts(xr3, blk_e, n_used, w1.astype(BF16), w3.astype(BF16), w2.astype(BF16), MOE_ROWS)
    return moe_combine(x, yr3, pos, gates.T, ln_g, ln_b, alpha)


def kernel(x_prompt, x_sample, a_w_qkv, a_w_o, b_w_in, b_lam_re, b_lam_im, b_log_dt, b_b_re, b_b_im, b_c_re,
           b_c_im, b_d, b_w_glu, c_w_in, c_lower_bounds, c_norm_g, c_w_o, router_w, router_bias, moe_w1, moe_w3,
           moe_w2, ln1_g, ln1_b, ln2_g, ln2_b):
    D = x_prompt.shape[-1]
    depth = ln1_g.shape[0]
    alpha = (2 * depth) ** 0.25
    seq_lens = (x_prompt.shape[1],) * x_prompt.shape[0] + (x_sample.shape[1],) * x_sample.shape[0]
    x = jnp.concatenate([x_prompt.reshape(-1, D), x_sample.reshape(-1, D)], 0).astype(F32)
    xb = x.astype(BF16)
    sm = jax.nn.softmax(c_lower_bounds.astype(F32), axis=0)
    lower_bounds = jnp.cumsum(sm, axis=0) - sm[0]
    for i in range(depth):
        j, kind = divmod(i, N_MIXERS)
        if kind == 0:
            h = dilated_attention_mixer(xb, a_w_qkv[j], a_w_o[j], seq_lens)
        elif kind == 1:
            h = s5_mixer(xb, b_w_in[j], b_lam_re[j], b_lam_im[j], b_log_dt[j], b_b_re[j], b_b_im[j], b_c_re[j],
                         b_c_im[j], b_d[j], b_w_glu[j], seq_lens)
        else:
            h = hgrn2_mixer(xb, c_w_in[j], lower_bounds[i], c_norm_g[j], c_w_o[j], seq_lens)
        x, xb = residual_layer_norm(x, h, ln1_g[i], ln1_b[i], alpha)
        x, xb = moe_block(x, xb, router_w, router_bias, moe_w1[i], moe_w3[i], moe_w2[i], ln2_g[i], ln2_b[i], alpha)
    n_prompt = x_prompt.shape[0] * x_prompt.shape[1]
    return (x[:n_prompt].reshape(x_prompt.shape).astype(x_prompt.dtype),
            x[n_prompt:].reshape(x_sample.shape).astype(x_sample.dtype))
```

```python
import functools
import math

import jax
import jax.numpy as jnp
import numpy as np
from jax import lax
from jax.experimental import pallas as pl
from jax.experimental.pallas import tpu as pltpu

F32 = jnp.float32
BF16 = jnp.bfloat16

LANES = 128
SUBLANES = 8
VMEM_LIMIT = 56 * 1024 * 1024

A_WINDOWS = (128, 512, 2048)
A_DILATIONS = (1, 4, 16)
A_HEAD_DIM = 128
ROPE_THETA = 10000.0
S5_CH = 16
S5_STATE = 64
S5_MAX_RE = -1e-4
C_KDIM = 128
C_CHUNK = 32
N_EXPERTS = 32
N_EXPERT_GROUPS = 8
EXPERTS_PER_GROUP = N_EXPERTS // N_EXPERT_GROUPS
TOP_K = 2
LN_EPS = 1e-5
RMS_EPS = 1e-6
N_MIXERS = 3
NEG_BIG = -1e30


def _tile(n, pref, unit=LANES):
    if n <= pref:
        return n
    t = pref - pref % unit
    while n % t:
        t -= unit
    return t


def _cparams(n_axes):
    return pltpu.CompilerParams(dimension_semantics=("arbitrary",) * n_axes, vmem_limit_bytes=VMEM_LIMIT)


def _mm_body(x_ref, w_ref, o_ref):
    acc = jnp.dot(x_ref[...].astype(BF16), w_ref[...].astype(BF16), preferred_element_type=F32)
    o_ref[...] = acc.astype(o_ref.dtype)


def matmul(x, w, out_dtype, tm=1024, tn=1024):
    M, K = x.shape
    N = w.shape[1]
    tm, tn = _tile(M, tm, SUBLANES), _tile(N, tn)
    return pl.pallas_call(
        _mm_body,
        grid=(M // tm, N // tn),
        in_specs=[pl.BlockSpec((tm, K), lambda i, j: (i, 0)), pl.BlockSpec((K, tn), lambda i, j: (0, j))],
        out_specs=pl.BlockSpec((tm, tn), lambda i, j: (i, j)),
        out_shape=jax.ShapeDtypeStruct((M, N), out_dtype),
        compiler_params=_cparams(2),
        name="matmul",
    )(x, w)


def _mm_glu_body(x_ref, wv_ref, wg_ref, o_ref):
    x = x_ref[...].astype(BF16)
    val = jnp.dot(x, wv_ref[...].astype(BF16), preferred_element_type=F32)
    gate = jnp.dot(x, wg_ref[...].astype(BF16), preferred_element_type=F32)
    o_ref[...] = (val * jax.nn.sigmoid(gate)).astype(o_ref.dtype)


def matmul_glu(x, w, out_dtype, tm=1024, tn=512):
    M, K = x.shape
    N = w.shape[1] // 2
    tm, tn = _tile(M, tm, SUBLANES), _tile(N, tn)
    nj = N // tn
    return pl.pallas_call(
        _mm_glu_body,
        grid=(M // tm, nj),
        in_specs=[
            pl.BlockSpec((tm, K), lambda i, j: (i, 0)),
            pl.BlockSpec((K, tn), lambda i, j: (0, j)),
            pl.BlockSpec((K, tn), lambda i, j: (0, j + nj)),
        ],
        out_specs=pl.BlockSpec((tm, tn), lambda i, j: (i, j)),
        out_shape=jax.ShapeDtypeStruct((M, N), out_dtype),
        compiler_params=_cparams(2),
        name="matmul_glu",
    )(x, w, w)


def _layer_norm_rows(y, g, b):
    yc = y - jnp.mean(y, -1, keepdims=True)
    var = jnp.mean(yc * yc, -1, keepdims=True)
    return yc * lax.rsqrt(var + LN_EPS) * g + b


def _res_ln_body(x_ref, h_ref, g_ref, b_ref, o_ref, ob_ref, *, alpha):
    y = alpha * x_ref[...] + h_ref[...].astype(F32)
    out = _layer_norm_rows(y, g_ref[...], b_ref[...])
    o_ref[...] = out
    ob_ref[...] = out.astype(BF16)


def residual_layer_norm(x, h, g, b, alpha, tm=512):
    M, D = x.shape
    tm = _tile(M, tm, SUBLANES)
    row = pl.BlockSpec((tm, D), lambda i: (i, 0))
    vec = pl.BlockSpec((1, D), lambda i: (0, 0))
    return pl.pallas_call(
        functools.partial(_res_ln_body, alpha=alpha),
        grid=(M // tm,),
        in_specs=[row, row, vec, vec],
        out_specs=[row, row],
        out_shape=[jax.ShapeDtypeStruct((M, D), F32), jax.ShapeDtypeStruct((M, D), BF16)],
        compiler_params=_cparams(1),
        name="residual_layer_norm",
    )(x, h, g.reshape(1, D).astype(F32), b.reshape(1, D).astype(F32))


def _residue_major(n, dil):
    p = np.arange(n)
    src = (p % (n // dil)) * dil + p // (n // dil)
    return (src[:, None] == np.arange(n)[None, :]).astype(np.float32)


def _qkv_body(x_ref, w_ref, cos_ref, sin_ref, *rest, dil, n_rope_tiles):
    if dil > 1:
        perm_ref, o_ref, xp_ref = rest
    else:
        o_ref, xp_ref = rest
    j = pl.program_id(1)
    tm = x_ref.shape[0]

    @pl.when(j == 0)
    def _():
        if dil > 1:
            xp_ref[...] = jnp.dot(perm_ref[...], x_ref[...], preferred_element_type=F32).astype(BF16)
        else:
            xp_ref[...] = x_ref[...]

    acc = jnp.dot(xp_ref[...], w_ref[...], preferred_element_type=F32)
    rows = tm // dil

    @pl.when(j < n_rope_tiles)
    def _():
        cos = cos_ref[...].reshape(tm, A_HEAD_DIM)
        sin = sin_ref[...].reshape(tm, A_HEAD_DIM)
        for h in range(acc.shape[1] // A_HEAD_DIM):
            hs = slice(h * A_HEAD_DIM, (h + 1) * A_HEAD_DIM)
            t = acc[:, hs]
            t = (t * cos + pltpu.roll(t, A_HEAD_DIM // 2, 1) * sin).astype(o_ref.dtype)
            for r in range(dil):
                o_ref[r, :, hs] = t[r * rows:(r + 1) * rows]

    @pl.when(j >= n_rope_tiles)
    def _():
        for r in range(dil):
            o_ref[r] = acc[r * rows:(r + 1) * rows].astype(o_ref.dtype)


def project_qkv(xb, w, cos, sin, dil, tm=1024, tn=1024):
    T, K = xb.shape
    N = w.shape[1]
    tm = _tile(T, tm, SUBLANES * 2 * dil)
    tn = _tile(N // 3, tn)
    in_specs = [
        pl.BlockSpec((tm, K), lambda i, j: (i, 0)),
        pl.BlockSpec((K, tn), lambda i, j: (0, j)),
        pl.BlockSpec((dil, tm // dil, A_HEAD_DIM), lambda i, j: (0, i, 0)),
        pl.BlockSpec((dil, tm // dil, A_HEAD_DIM), lambda i, j: (0, i, 0)),
    ]
    args = [xb, w, cos, sin]
    if dil > 1:
        in_specs.append(pl.BlockSpec((tm, tm), lambda i, j: (0, 0)))
        args.append(jnp.asarray(_residue_major(tm, dil), BF16))
    return pl.pallas_call(
        functools.partial(_qkv_body, dil=dil, n_rope_tiles=(2 * N // 3) // tn),
        grid=(T // tm, N // tn),
        in_specs=in_specs,
        out_specs=pl.BlockSpec((dil, tm // dil, tn), lambda i, j: (0, i, j)),
        out_shape=jax.ShapeDtypeStruct((dil, T // dil, N), BF16),
        scratch_shapes=[pltpu.VMEM((tm, K), BF16)],
        compiler_params=_cparams(2),
        name=f"project_qkv_d{dil}",
    )(*args)


def _segment_bounds(seq_lens, dil, tile):
    lo, hi = [], []
    start = 0
    for _ in range(dil):
        for s in seq_lens:
            n = s // dil
            assert s % dil == 0 and n % tile == 0, (s, dil, tile)
            lo += [start] * (n // tile)
            hi += [start + n] * (n // tile)
            start += n
    return np.asarray(lo, np.int32), np.asarray(hi, np.int32)


def _attn_body(lo_ref, hi_ref, q_ref, k_ref, kp_ref, kn_ref, v_ref, vp_ref, vn_ref, o_ref, lse_ref,
               kbuf, vbuf, *, tq, nh, half, scale):
    i = pl.program_id(0)
    lo = lo_ref[i]
    hi = hi_ref[i]
    sb = 2 * half
    kbuf[0:half] = kp_ref[...]
    kbuf[half:half + tq] = k_ref[...]
    kbuf[half + tq:] = kn_ref[...]
    vbuf[0:half] = vp_ref[...]
    vbuf[half:half + tq] = v_ref[...]
    vbuf[half + tq:] = vn_ref[...]
    r = lax.broadcasted_iota(jnp.int32, (sb, 2 * sb), 0)
    c = lax.broadcasted_iota(jnp.int32, (sb, 2 * sb), 1)
    band = (c >= r) & (c <= r + 2 * half)
    lane = lax.broadcasted_iota(jnp.int32, (sb, LANES), 1)
    for j in range(tq // sb):
        kpos = i * tq + (j * sb - half) + c
        valid = band & (kpos >= lo) & (kpos < hi)
        lse_tile = jnp.zeros((sb, LANES), F32)
        for h in range(nh):
            hs = slice(h * A_HEAD_DIM, (h + 1) * A_HEAD_DIM)
            qj = q_ref[j * sb:(j + 1) * sb, hs]
            kj = kbuf[j * sb:(j + 2) * sb, hs]
            vj = vbuf[j * sb:(j + 2) * sb, hs]
            s = lax.dot_general(qj, kj, (((1,), (1,)), ((), ())), preferred_element_type=F32) * scale
            s = jnp.where(valid, s, NEG_BIG)
            m = jnp.max(s, -1, keepdims=True)
            p = jnp.exp(s - m)
            den = jnp.sum(p, -1, keepdims=True)
            o = jnp.dot(p.astype(BF16), vj, preferred_element_type=F32) / den
            o_ref[j * sb:(j + 1) * sb, hs] = o.astype(o_ref.dtype)
            lse_tile = jnp.where(lane == h, m + jnp.log(den), lse_tile)
        lse_ref[j * sb:(j + 1) * sb, :] = lse_tile


def dilated_attention_group(qkv, g, seq_lens, tq=256):
    T = qkv.shape[0]
    W = qkv.shape[1] // 3
    nh = W // A_HEAD_DIM
    dil = A_DILATIONS[g]
    half = A_WINDOWS[g] // (2 * dil)
    tq = min(tq, min(seq_lens) // max(A_DILATIONS))
    assert tq % (2 * half) == 0 and nh <= LANES
    n_tiles = T // tq
    hb = tq // half
    n_hblk = T // half
    lo, hi = _segment_bounds(seq_lens, dil, tq)
    prev = lambda i: jnp.maximum(i * hb - 1, 0)
    nxt = lambda i: jnp.minimum((i + 1) * hb, n_hblk - 1)
    in_specs = [
        pl.BlockSpec((tq, W), lambda i, lo, hi: (i, 0)),
        pl.BlockSpec((tq, W), lambda i, lo, hi: (i, 1)),
        pl.BlockSpec((half, W), lambda i, lo, hi: (prev(i), 1)),
        pl.BlockSpec((half, W), lambda i, lo, hi: (nxt(i), 1)),
        pl.BlockSpec((tq, W), lambda i, lo, hi: (i, 2)),
        pl.BlockSpec((half, W), lambda i, lo, hi: (prev(i), 2)),
        pl.BlockSpec((half, W), lambda i, lo, hi: (nxt(i), 2)),
    ]
    out_specs = [
        pl.BlockSpec((tq, W), lambda i, lo, hi: (i, 0)),
        pl.BlockSpec((tq, LANES), lambda i, lo, hi: (i, 0)),
    ]
    return pl.pallas_call(
        functools.partial(_attn_body, tq=tq, nh=nh, half=half, scale=A_HEAD_DIM ** -0.5),
        grid_spec=pltpu.PrefetchScalarGridSpec(
            num_scalar_prefetch=2,
            grid=(n_tiles,),
            in_specs=in_specs,
            out_specs=out_specs,
            scratch_shapes=[pltpu.VMEM((tq + 2 * half, W), BF16), pltpu.VMEM((tq + 2 * half, W), BF16)],
        ),
        out_shape=[jax.ShapeDtypeStruct((T, W), BF16), jax.ShapeDtypeStruct((T, LANES), F32)],
        compiler_params=_cparams(1),
        name=f"dilated_attention_g{g}",
    )(jnp.asarray(lo), jnp.asarray(hi), qkv, qkv, qkv, qkv, qkv, qkv, qkv)


def _attn_out_body(*refs, dils, nh, alpha):
    ng = len(dils)
    o_refs, l_refs = refs[:ng], refs[ng:2 * ng]
    pt_refs = refs[2 * ng:2 * ng + sum(d > 1 for d in dils)]
    wo_ref, x_ref, g_ref, b_ref, out_ref, outb_ref = refs[2 * ng + len(pt_refs):]
    tm = x_ref.shape[0]
    os_, ls = [], []
    k = 0
    for g, d in enumerate(dils):
        o = o_refs[g][...].reshape(tm, nh * A_HEAD_DIM)
        l = l_refs[g][...].reshape(tm, LANES)
        if d > 1:
            pt = pt_refs[k][...]
            k += 1
            o = jnp.dot(pt, o, preferred_element_type=F32)
            l = functools.reduce(jnp.add, [jnp.dot(pt, part, preferred_element_type=F32)
                                           for part in _split3_bf16(l)])
        os_.append(o)
        ls.append(l)
    m = functools.reduce(jnp.maximum, ls)
    es = [jnp.exp(l - m) for l in ls]
    den = functools.reduce(jnp.add, es)
    ws = [e / den for e in es]
    mixed = []
    for h in range(nh):
        hs = slice(h * A_HEAD_DIM, (h + 1) * A_HEAD_DIM)
        acc = ws[0][:, h:h + 1] * os_[0][:, hs]
        for g in range(1, ng):
            acc = acc + ws[g][:, h:h + 1] * os_[g][:, hs]
        mixed.append(acc.astype(BF16))
    hmix = jnp.dot(jnp.concatenate(mixed, axis=1), wo_ref[...], preferred_element_type=F32)
    out = _layer_norm_rows(alpha * x_ref[...] + hmix, g_ref[...], b_ref[...])
    out_ref[...] = out
    outb_ref[...] = out.astype(BF16)


def attention_output(x, outs, lses, w_o, ln_g, ln_b, alpha, tm=256):
    T, D = x.shape
    W = outs[0].shape[1]
    dils = A_DILATIONS
    tm = _tile(T, tm, SUBLANES * 2 * max(dils))
    in_specs, args = [], []
    for arrs, width in ((outs, W), (lses, LANES)):
        for a, d in zip(arrs, dils):
            in_specs.append(pl.BlockSpec((d, tm // d, width), lambda i: (0, i, 0)))
            args.append(a.reshape(d, T // d, width))
    for d in dils:
        if d > 1:
            in_specs.append(pl.BlockSpec((tm, tm), lambda i: (0, 0)))
            args.append(jnp.asarray(_residue_major(tm, d).T, BF16))
    row = pl.BlockSpec((tm, D), lambda i: (i, 0))
    vec = pl.BlockSpec((1, D), lambda i: (0, 0))
    in_specs += [pl.BlockSpec((W, D), lambda i: (0, 0)), row, vec, vec]
    args += [w_o, x, ln_g.reshape(1, D).astype(F32), ln_b.reshape(1, D).astype(F32)]
    return pl.pallas_call(
        functools.partial(_attn_out_body, dils=dils, nh=W // A_HEAD_DIM, alpha=alpha),
        grid=(T // tm,),
        in_specs=in_specs,
        out_specs=[row, row],
        out_shape=[jax.ShapeDtypeStruct((T, D), F32), jax.ShapeDtypeStruct((T, D), BF16)],
        compiler_params=_cparams(1),
        name="attention_output",
    )(*args)


def _rope_tables(seq_lens, dil):
    half = A_HEAD_DIM // 2
    inv_freq = ROPE_THETA ** (-jnp.arange(half, dtype=F32) / half)
    pos = jnp.concatenate([jnp.arange(s, dtype=F32) for s in seq_lens])
    ang = pos[:, None] * inv_freq[None, :]
    cos, sin = jnp.cos(ang), jnp.sin(ang)
    regroup = lambda t: t.reshape(-1, dil, A_HEAD_DIM).transpose(1, 0, 2)
    return regroup(jnp.concatenate([cos, cos], -1)), regroup(jnp.concatenate([-sin, sin], -1))


def dilated_attention_layer(x, xb, w_qkv, w_o, ln_g, ln_b, alpha, seq_lens):
    n_groups = len(A_WINDOWS)
    T, D = x.shape
    W = w_qkv.shape[1] // (3 * n_groups)
    wb = w_qkv.astype(BF16).reshape(D, 3, n_groups, W)
    outs, lses = [], []
    for g in range(n_groups):
        dil = A_DILATIONS[g]
        cos, sin = _rope_tables(seq_lens, dil)
        qkv = project_qkv(xb, wb[:, :, g].reshape(D, 3 * W), cos, sin, dil)
        o, lse = dilated_attention_group(qkv.reshape(T, 3 * W), g, seq_lens)
        outs.append(o)
        lses.append(lse)
    return attention_output(x, outs, lses, w_o.astype(BF16), ln_g, ln_b, alpha)


S5_GROUP_BLOCK = LANES // S5_CH
S5_NS = S5_GROUP_BLOCK * S5_STATE
S5_STREAMS = SUBLANES


def _cmul(ar, ai, br, bi):
    return ar * br - ai * bi, ar * bi + ai * br


def _split3_bf16(x):
    hi = x.astype(BF16)
    r1 = x - hi.astype(F32)
    mid = r1.astype(BF16)
    lo = (r1 - mid.astype(F32)).astype(BF16)
    return hi, mid, lo


def _s5_body(reset_ref, u_ref, d_ref, bblk_ref, cblk_ref, lam_ref, p_ref, pt_ref, *rest, tt, rev, final):
    if final:
        yin_ref, o_ref, bu_ref, carry_ref = rest
    else:
        o_ref, bu_ref, carry_ref = rest
    i = pl.program_id(1)
    ns = S5_NS
    nc = ns // LANES
    lc = tt // S5_STREAMS
    ss = S5_STREAMS

    @pl.when(reset_ref[i] == 1)
    def _():
        carry_ref[...] = jnp.zeros_like(carry_ref)

    perm = p_ref[...]
    u = functools.reduce(jnp.add, [jnp.dot(perm, part, preferred_element_type=F32)
                                   for part in _split3_bf16(u_ref[...])])
    bu_ref[...] = jnp.dot(u.astype(BF16), bblk_ref[0], preferred_element_type=F32)
    lam = lam_ref[0]
    order = range(lc - 1, -1, -1) if rev else range(lc)
    row = lax.broadcasted_iota(jnp.int32, (ss, LANES), 0)
    last = 0 if rev else ss - 1
    first = ss - 1 if rev else 0
    shift1 = ss - 1 if rev else 1
    re_cols = [slice(c * LANES, (c + 1) * LANES) for c in range(nc)]
    im_cols = [slice(ns + c * LANES, ns + (c + 1) * LANES) for c in range(nc)]
    rows = lambda k, c: jnp.broadcast_to(lam[k:k + 1, re_cols[c]], (ss, LANES))
    lr = [rows(0, c) for c in range(nc)]
    li = [rows(1, c) for c in range(nc)]

    hr = [jnp.zeros((ss, LANES), F32) for _ in range(nc)]
    hi = [jnp.zeros((ss, LANES), F32) for _ in range(nc)]
    for j in order:
        js = slice(j * ss, (j + 1) * ss)
        for c in range(nc):
            pr, pi = _cmul(lr[c], li[c], hr[c], hi[c])
            hr[c], hi[c] = pr + bu_ref[js, re_cols[c]], pi + bu_ref[js, im_cols[c]]

    for c in range(nc):
        er, ei = hr[c], hi[c]
        cr = jnp.broadcast_to(carry_ref[last:last + 1, re_cols[c]], (ss, LANES))
        ci = jnp.broadcast_to(carry_ref[last:last + 1, im_cols[c]], (ss, LANES))
        sr = jnp.where(row == first, cr, pltpu.roll(er, shift1, 0))
        si = jnp.where(row == first, ci, pltpu.roll(ei, shift1, 0))
        for n, k in enumerate((1, 2, 4)):
            keep = (row <= ss - 1 - k) if rev else (row >= k)
            shift = ss - k if rev else k
            qr = jnp.where(keep, pltpu.roll(sr, shift, 0), 0.0)
            qi = jnp.where(keep, pltpu.roll(si, shift, 0), 0.0)
            pr, pi = _cmul(rows(2 + 2 * n, c), rows(3 + 2 * n, c), qr, qi)
            sr, si = sr + pr, si + pi
        pr, pi = _cmul(rows(2, c), rows(3, c), sr, si)
        carry_ref[:, re_cols[c]] = pr + er
        carry_ref[:, im_cols[c]] = pi + ei
        hr[c], hi[c] = sr, si

    for j in order:
        js = slice(j * ss, (j + 1) * ss)
        for c in range(nc):
            pr, pi = _cmul(lr[c], li[c], hr[c], hi[c])
            hr[c], hi[c] = pr + bu_ref[js, re_cols[c]], pi + bu_ref[js, im_cols[c]]
            bu_ref[js, re_cols[c]] = hr[c]
            bu_ref[js, im_cols[c]] = hi[c]

    y = jnp.dot(bu_ref[...].astype(BF16), cblk_ref[0], preferred_element_type=F32)
    if final:
        y = jax.nn.gelu(yin_ref[...] + y).astype(BF16)
        o_ref[...] = jnp.dot(pt_ref[...], y, preferred_element_type=F32).astype(o_ref.dtype)
    else:
        o_ref[...] = u * d_ref[...] + y


def _s5_direction_params(lam_re, lam_im, log_dt, b_re, b_im, c_re, c_im, lc):
    G = lam_re.shape[0]
    nb = G // S5_GROUP_BLOCK
    a = jnp.minimum(lam_re.astype(F32), S5_MAX_RE)
    b = lam_im.astype(F32)
    dt = jnp.exp(log_dt.astype(F32))[:, None]

    def lam_pow(n):
        mag = jnp.exp(n * a * dt)
        return mag * jnp.cos(n * b * dt), mag * jnp.sin(n * b * dt)

    lbr, lbi = lam_pow(1.0)
    nr, ni = lbr - 1.0, lbi
    den = a * a + b * b
    fr, fi = (nr * a + ni * b) / den, (ni * a - nr * b) / den
    bbr = fr[..., None] * b_re - fi[..., None] * b_im
    bbi = fr[..., None] * b_im + fi[..., None] * b_re
    eye = jnp.eye(S5_GROUP_BLOCK, dtype=F32)

    def in_block(t):
        t = t.reshape(nb, S5_GROUP_BLOCK, S5_STATE, S5_CH)
        return jnp.einsum('ngpc,gh->ngchp', t, eye).reshape(nb, LANES, S5_NS)

    def out_block(t):
        t = t.reshape(nb, S5_GROUP_BLOCK, S5_CH, S5_STATE)
        return jnp.einsum('ngcp,gh->nhpgc', t, eye).reshape(nb, S5_NS, LANES)

    bblk = jnp.concatenate([in_block(bbr), in_block(bbi)], -1).astype(BF16)
    cblk = jnp.concatenate([out_block(c_re.astype(F32)), out_block(-c_im.astype(F32))], 1).astype(BF16)
    vecs = []
    for n in (1.0, float(lc), 2.0 * lc, 4.0 * lc):
        vecs += list(lam_pow(n))
    lam_rows = jnp.stack([v.reshape(nb, S5_NS) for v in vecs], 1)
    return bblk, cblk, lam_rows


def _s5_scan(u, d_skip, params, y_in, seq_lens, rev, tt):
    T, D = u.shape
    nb = D // LANES
    nt = T // tt
    bblk, cblk, lam_rows = params
    starts = np.cumsum([0] + list(seq_lens))[:-1] // tt
    ends = np.cumsum(list(seq_lens)) // tt - 1
    reset = np.zeros((nt,), np.int32)
    reset[(nt - 1 - ends) if rev else starts] = 1
    tmap = (lambda i: nt - 1 - i) if rev else (lambda i: i)
    final = y_in is not None
    tile = pl.BlockSpec((tt, LANES), lambda b, i, rs: (tmap(i), b))
    in_specs = [
        tile,
        pl.BlockSpec((1, LANES), lambda b, i, rs: (0, b)),
        pl.BlockSpec((1, LANES, 2 * S5_NS), lambda b, i, rs: (b, 0, 0)),
        pl.BlockSpec((1, 2 * S5_NS, LANES), lambda b, i, rs: (b, 0, 0)),
        pl.BlockSpec((1, SUBLANES, S5_NS), lambda b, i, rs: (b, 0, 0)),
        pl.BlockSpec((tt, tt), lambda b, i, rs: (0, 0)),
        pl.BlockSpec((tt, tt), lambda b, i, rs: (0, 0)),
    ]
    src = (np.arange(tt) % S5_STREAMS) * (tt // S5_STREAMS) + np.arange(tt) // S5_STREAMS
    perm = (src[:, None] == np.arange(tt)[None, :]).astype(np.float32)
    args = [u, d_skip.reshape(1, D).astype(F32), bblk, cblk, lam_rows,
            jnp.asarray(perm, BF16), jnp.asarray(perm.T, BF16)]
    if final:
        in_specs.append(tile)
        args.append(y_in)
    return pl.pallas_call(
        functools.partial(_s5_body, tt=tt, rev=rev, final=final),
        grid_spec=pltpu.PrefetchScalarGridSpec(
            num_scalar_prefetch=1,
            grid=(nb, nt),
            in_specs=in_specs,
            out_specs=tile,
            scratch_shapes=[pltpu.VMEM((tt, 2 * S5_NS), F32), pltpu.VMEM((S5_STREAMS, 2 * S5_NS), F32)],
        ),
        out_shape=jax.ShapeDtypeStruct((T, D), BF16 if final else F32),
        compiler_params=_cparams(2),
        name="s5_scan_rev" if rev else "s5_scan_fwd",
    )(jnp.asarray(reset), *args)


def s5_mixer(xb, w_in, lam_re, lam_im, log_dt, b_re, b_im, c_re, c_im, d_skip, w_glu, seq_lens, tt=512):
    tt = min(tt, min(seq_lens))
    assert all(s % tt == 0 for s in seq_lens) and tt % S5_STREAMS == 0
    u = matmul(xb, w_in.astype(BF16), F32)
    lc = tt // S5_STREAMS
    prm = [_s5_direction_params(lam_re[r], lam_im[r], log_dt[r], b_re[r], b_im[r], c_re[r], c_im[r], lc)
           for r in range(2)]
    y = _s5_scan(u, d_skip, prm[0], None, seq_lens, False, tt)
    y = _s5_scan(u, d_skip, prm[1], y, seq_lens, True, tt)
    return matmul_glu(y, w_glu.astype(BF16), F32)


def _hgrn_body(reset_ref, zq_ref, zf_ref, zi_ref, lb_ref, *rest, tt, nh, rev, final):
    if final:
        zg_ref, ofwd_ref, ng_ref, o_ref, st_ref = rest
    else:
        o_ref, st_ref = rest
    i = pl.program_id(1)
    C = C_CHUNK

    @pl.when(reset_ref[i] == 1)
    def _():
        st_ref[...] = jnp.zeros_like(st_ref)

    r = lax.broadcasted_iota(jnp.int32, (tt, tt), 0)
    c = lax.broadcasted_iota(jnp.int32, (tt, tt), 1)
    same = (r // C) == (c // C)
    cum_m = same & ((c >= r) if rev else (c <= r))
    sum_m = jnp.concatenate([cum_m, same], 0).astype(BF16)
    lb = lb_ref[...]
    f = lb + (1.0 - lb) * jax.nn.sigmoid(zf_ref[...])
    k = 1.0 - f
    sums = functools.reduce(jnp.add, [jnp.dot(sum_m, part, preferred_element_type=F32)
                                      for part in _split3_bf16(jnp.log(f))])
    bcum, tot = sums[:tt], sums[tt:]
    qd = (jax.nn.silu(zq_ref[...]) * jnp.exp(bcum)).astype(BF16)
    ki = (k * jnp.exp(-bcum)).astype(BF16)
    ke = (k * jnp.exp(tot - bcum)).astype(BF16)
    v = zi_ref[...].astype(BF16)
    dec = jnp.exp(tot)
    rr = lax.broadcasted_iota(jnp.int32, (C, C), 0)
    cc = lax.broadcasted_iota(jnp.int32, (C, C), 1)
    tri = (cc >= rr) if rev else (cc <= rr)
    nt_dims = (((1,), (1,)), ((), ()))
    chunks = range(tt // C - 1, -1, -1) if rev else range(tt // C)
    for n in chunks:
        rs = slice(n * C, (n + 1) * C)
        for h in range(nh):
            hs = slice(h * C_KDIM, (h + 1) * C_KDIM)
            qh, vh = qd[rs, hs], v[rs, hs]
            att = lax.dot_general(qh, ki[rs, hs], nt_dims, preferred_element_type=F32)
            att = jnp.where(tri, att, 0.0).astype(BF16)
            st = st_ref[h]
            o = (jnp.dot(att, vh, preferred_element_type=F32)
                 + lax.dot_general(qh, st.astype(BF16), nt_dims, preferred_element_type=F32))
            st_ref[h] = st * dec[n * C:n * C + 1, hs] + jnp.dot(vh.T, ke[rs, hs], preferred_element_type=F32)
            if final:
                o = o + ofwd_ref[rs, hs]
                o = o * lax.rsqrt(jnp.mean(o * o, -1, keepdims=True) + RMS_EPS) * ng_ref[...]
                o = o * jax.nn.silu(zg_ref[rs, hs])
            o_ref[rs, hs] = o.astype(o_ref.dtype)


def _hgrn_scan(z, lb, seq_lens, rev, tt, nh, o_fwd=None, norm_g=None):
    T = z.shape[0]
    D = z.shape[1] // 5
    W = nh * C_KDIM
    nhb = D // W
    nt = T // tt
    starts = np.cumsum([0] + list(seq_lens))[:-1] // tt
    ends = np.cumsum(list(seq_lens)) // tt - 1
    reset = np.zeros((nt,), np.int32)
    reset[(nt - 1 - ends) if rev else starts] = 1
    tmap = (lambda i: nt - 1 - i) if rev else (lambda i: i)
    final = o_fwd is not None
    zcol = lambda n: pl.BlockSpec((tt, W), lambda b, i, rs: (tmap(i), n * nhb + b))
    in_specs = [zcol(0), zcol(2 if rev else 1), zcol(3), pl.BlockSpec((1, W), lambda b, i, rs: (0, b))]
    args = [z, z, z, lb.reshape(1, D).astype(F32)]
    if final:
        in_specs += [zcol(4), pl.BlockSpec((tt, W), lambda b, i, rs: (tmap(i), b)),
                     pl.BlockSpec((1, C_KDIM), lambda b, i, rs: (0, 0))]
        args += [z, o_fwd, norm_g.reshape(1, C_KDIM).astype(F32)]
    return pl.pallas_call(
        functools.partial(_hgrn_body, tt=tt, nh=nh, rev=rev, final=final),
        grid_spec=pltpu.PrefetchScalarGridSpec(
            num_scalar_prefetch=1,
            grid=(nhb, nt),
            in_specs=in_specs,
            out_specs=pl.BlockSpec((tt, W), lambda b, i, rs: (tmap(i), b)),
            scratch_shapes=[pltpu.VMEM((nh, C_KDIM, C_KDIM), F32)],
        ),
        out_shape=jax.ShapeDtypeStruct((T, D), BF16 if final else F32),
        compiler_params=_cparams(2),
        name="hgrn2_scan_rev" if rev else "hgrn2_scan_fwd",
    )(jnp.asarray(reset), *args)


def hgrn2_mixer(xb, w_in, lb, norm_g, w_o, seq_lens, tt=128, nh=16):
    assert all(s % tt == 0 for s in seq_lens) and tt % C_CHUNK == 0
    nh = min(nh, xb.shape[1] // C_KDIM)
    z = matmul(xb, w_in.astype(BF16), F32)
    o_fwd = _hgrn_scan(z, lb, seq_lens, False, tt, nh)
    o = _hgrn_scan(z, lb, seq_lens, True, tt, nh, o_fwd, norm_g)
    return matmul(o, w_o.astype(BF16), F32)


MOE_ROWS = 256


def _token_slab(d):
    return (d // LANES, LANES)


def _router_body(x_ref, wt_ref, bias_ref, eidx_ref, gate_ref):
    x = x_ref[...]
    wt = wt_ref[...]
    xh = x.astype(BF16)
    xl = (x - xh.astype(F32)).astype(BF16)
    wh = wt.astype(BF16)
    wl = (wt - wh.astype(F32)).astype(BF16)
    nt_dims = (((1,), (1,)), ((), ()))
    dg = lambda a, b: lax.dot_general(a, b, nt_dims, preferred_element_type=F32)
    scores = jax.nn.sigmoid(dg(wh, xh) + dg(wh, xl) + dg(wl, xh))
    sel = scores + bias_ref[...]
    ng = N_EXPERT_GROUPS
    s = [sel[j * ng:(j + 1) * ng] for j in range(EXPERTS_PER_GROUP)]
    sc = [scores[j * ng:(j + 1) * ng] for j in range(EXPERTS_PER_GROUP)]
    hi1, lo1 = jnp.maximum(s[0], s[1]), jnp.minimum(s[0], s[1])
    hi2, lo2 = jnp.maximum(s[2], s[3]), jnp.minimum(s[2], s[3])
    gs = jnp.maximum(hi1, hi2) + jnp.maximum(jnp.minimum(hi1, hi2), jnp.maximum(lo1, lo2))
    gi = lax.broadcasted_iota(jnp.int32, gs.shape, 0)
    gidx = jnp.min(jnp.where(gs == jnp.max(gs, 0, keepdims=True), gi, ng), 0, keepdims=True)
    gm = gi == gidx
    val = [jnp.sum(jnp.where(gm, t, 0.0), 0, keepdims=True) for t in s]
    scv = [jnp.sum(jnp.where(gm, t, 0.0), 0, keepdims=True) for t in sc]
    best, bj = val[0], jnp.zeros_like(gidx)
    for j in range(1, EXPERTS_PER_GROUP):
        upd = val[j] > best
        best, bj = jnp.where(upd, val[j], best), jnp.where(upd, j, bj)
    sec, sj = jnp.full_like(best, -jnp.inf), jnp.zeros_like(gidx)
    for j in range(EXPERTS_PER_GROUP):
        upd = (bj != j) & (val[j] > sec)
        sec, sj = jnp.where(upd, val[j], sec), jnp.where(upd, j, sj)
    pick = lambda idx: functools.reduce(jnp.add, [jnp.where(idx == j, scv[j], 0.0)
                                                  for j in range(EXPERTS_PER_GROUP)])
    g1, g2 = pick(bj), pick(sj)
    eidx_ref[0:1, :] = gidx * EXPERTS_PER_GROUP + bj
    eidx_ref[1:2, :] = gidx * EXPERTS_PER_GROUP + sj
    gate_ref[0:1, :] = g1 / (g1 + g2)
    gate_ref[1:2, :] = g2 / (g1 + g2)


def moe_router(x, router_w, router_bias, tm=512):
    T, D = x.shape
    tm = _tile(T, tm)
    perm = np.arange(N_EXPERTS).reshape(N_EXPERT_GROUPS, EXPERTS_PER_GROUP).T.reshape(-1)
    wt = router_w.astype(F32).T[perm]
    bias = router_bias.astype(F32)[perm].reshape(N_EXPERTS, 1)
    return pl.pallas_call(
        _router_body,
        grid=(T // tm,),
        in_specs=[pl.BlockSpec((tm, D), lambda i: (i, 0)), pl.BlockSpec((N_EXPERTS, D), lambda i: (0, 0)),
                  pl.BlockSpec((N_EXPERTS, 1), lambda i: (0, 0))],
        out_specs=[pl.BlockSpec((TOP_K, tm), lambda i: (0, i)), pl.BlockSpec((TOP_K, tm), lambda i: (0, i))],
        out_shape=[jax.ShapeDtypeStruct((TOP_K, T), jnp.int32), jax.ShapeDtypeStruct((TOP_K, T), F32)],
        compiler_params=_cparams(1),
        name="moe_router",
    )(x, wt, bias)


def _dispatch_plan(eidx, blk):
    T = eidx.shape[1]
    A = T * TOP_K
    flat_e = eidx.T.reshape(A)
    onehot = (flat_e[:, None] == jnp.arange(N_EXPERTS, dtype=jnp.int32)[None, :]).astype(jnp.int32)
    ck = _tile(A, 256, SUBLANES)
    within = jnp.einsum('ts,nse->nte', jnp.tril(jnp.ones((ck, ck), BF16)),
                        onehot.astype(BF16).reshape(A // ck, ck, N_EXPERTS), preferred_element_type=F32)
    chunk_tot = within[:, -1, :]
    csum = (within + (jnp.cumsum(chunk_tot, axis=0) - chunk_tot)[:, None, :]).reshape(A, N_EXPERTS).astype(jnp.int32)
    counts = csum[-1]
    padded = (counts + blk - 1) // blk * blk
    pend = jnp.cumsum(padded)
    pstart = pend - padded
    pos = jnp.sum(onehot * (csum - 1 + pstart[None, :]), axis=1).astype(jnp.int32)
    n_blocks = A // blk + N_EXPERTS
    blk_e = jnp.minimum(jnp.searchsorted(pend, jnp.arange(n_blocks, dtype=jnp.int32) * blk, side='right'),
                        N_EXPERTS - 1).astype(jnp.int32)
    n_used = (pend[-1] // blk).astype(jnp.int32).reshape(1)
    return pos, blk_e, n_used, n_blocks


def _dispatch_body(pos_ref, x_ref, xr_in_ref, xr_ref, sem):
    del xr_in_ref
    tm = x_ref.shape[0]

    def copy(t, a):
        return pltpu.make_async_copy(x_ref.at[t], xr_ref.at[pos_ref[0, a]], sem)

    def issue(t, carry):
        for k in range(TOP_K):
            copy(t, TOP_K * t + k).start()
        return carry

    def drain(t, carry):
        for k in range(TOP_K):
            copy(t, TOP_K * t + k).wait()
        return carry

    lax.fori_loop(0, tm, issue, 0)
    lax.fori_loop(0, tm, drain, 0)


def moe_dispatch(x3, pos, rows, tm=256):
    T = x3.shape[0]
    tm = _tile(T, tm, SUBLANES)
    pos3 = pos.reshape(T // tm, 1, TOP_K * tm)
    return pl.pallas_call(
        _dispatch_body,
        grid=(T // tm,),
        in_specs=[pl.BlockSpec((None, 1, TOP_K * tm), lambda i: (i, 0, 0), memory_space=pltpu.SMEM),
                  pl.BlockSpec((tm,) + x3.shape[1:], lambda i: (i, 0, 0)),
                  pl.BlockSpec(memory_space=pl.ANY)],
        out_specs=pl.BlockSpec(memory_space=pl.ANY),
        out_shape=jax.ShapeDtypeStruct((rows,) + x3.shape[1:], x3.dtype),
        scratch_shapes=[pltpu.SemaphoreType.DMA(())],
        input_output_aliases={2: 0},
        compiler_params=_cparams(1),
        name="moe_dispatch",
    )(pos3, x3, jnp.zeros((rows,) + x3.shape[1:], x3.dtype))


def _slab_to_rows(ref):
    return jnp.concatenate([ref[:, j, :] for j in range(ref.shape[1])], axis=1)


def _expert_body(be_ref, nu_ref, x_ref, w1_ref, w3_ref, w2_ref, o_ref):
    del be_ref

    @pl.when(pl.program_id(0) < nu_ref[0])
    def _():
        x = _slab_to_rows(x_ref).astype(BF16)
        h1 = jnp.dot(x, w1_ref[0], preferred_element_type=F32)
        h3 = jnp.dot(x, w3_ref[0], preferred_element_type=F32)
        y = jnp.dot((jax.nn.silu(h1) * h3).astype(BF16), w2_ref[0], preferred_element_type=F32)
        w = o_ref.shape[2]
        for j in range(o_ref.shape[1]):
            o_ref[:, j, :] = y[:, j * w:(j + 1) * w]

    @pl.when(pl.program_id(0) >= nu_ref[0])
    def _():
        o_ref[...] = jnp.zeros_like(o_ref)


def moe_experts(xr3, blk_e, n_used, w1, w3, w2, blk):
    rows = xr3.shape[0]
    E, D, F = w1.shape
    slab = pl.BlockSpec((blk,) + xr3.shape[1:], lambda b, be, nu: (b, 0, 0))
    return pl.pallas_call(
        _expert_body,
        grid_spec=pltpu.PrefetchScalarGridSpec(
            num_scalar_prefetch=2,
            grid=(rows // blk,),
            in_specs=[slab,
                      pl.BlockSpec((1, D, F), lambda b, be, nu: (be[b], 0, 0)),
                      pl.BlockSpec((1, D, F), lambda b, be, nu: (be[b], 0, 0)),
                      pl.BlockSpec((1, F, D), lambda b, be, nu: (be[b], 0, 0))],
            out_specs=slab,
        ),
        out_shape=jax.ShapeDtypeStruct(xr3.shape, F32),
        compiler_params=_cparams(1),
        name="moe_experts",
    )(blk_e, n_used, xr3, w1, w3, w2)


def _combine_body(pos_ref, gate_ref, x_ref, g_ref, b_ref, yr_ref, o_ref, ob_ref, buf0, buf1, sem, *, alpha):
    tm = x_ref.shape[0]
    bufs = (buf0, buf1)

    def copy(t, k):
        return pltpu.make_async_copy(yr_ref.at[pos_ref[0, TOP_K * t + k]], bufs[k].at[t], sem)

    def issue(t, carry):
        for k in range(TOP_K):
            copy(t, k).start()
        return carry

    def drain(t, carry):
        for k in range(TOP_K):
            copy(t, k).wait()
        return carry

    lax.fori_loop(0, tm, issue, 0)
    lax.fori_loop(0, tm, drain, 0)
    gates = gate_ref[...]
    f = gates[:, 0:1] * _slab_to_rows(buf0) + gates[:, 1:2] * _slab_to_rows(buf1)
    out = _layer_norm_rows(alpha * x_ref[...] + f, g_ref[...], b_ref[...])
    o_ref[...] = out
    ob_ref[...] = out.astype(BF16)


def moe_combine(x, yr3, pos, gates, g, b, alpha, tm=256):
    T, D = x.shape
    tm = _tile(T, tm, SUBLANES)
    pos3 = pos.reshape(T // tm, 1, TOP_K * tm)
    row = pl.BlockSpec((tm, D), lambda i: (i, 0))
    vec = pl.BlockSpec((1, D), lambda i: (0, 0))
    return pl.pallas_call(
        functools.partial(_combine_body, alpha=alpha),
        grid=(T // tm,),
        in_specs=[pl.BlockSpec((None, 1, TOP_K * tm), lambda i: (i, 0, 0), memory_space=pltpu.SMEM),
                  pl.BlockSpec((tm, TOP_K), lambda i: (i, 0)), row, vec, vec,
                  pl.BlockSpec(memory_space=pl.ANY)],
        out_specs=[row, row],
        out_shape=[jax.ShapeDtypeStruct((T, D), F32), jax.ShapeDtypeStruct((T, D), BF16)],
        scratch_shapes=[pltpu.VMEM((tm,) + yr3.shape[1:], F32), pltpu.VMEM((tm,) + yr3.shape[1:], F32),
                        pltpu.SemaphoreType.DMA(())],
        compiler_params=_cparams(1),
        name="moe_combine",
    )(pos3, gates, x, g.reshape(1, D).astype(F32), b.reshape(1, D).astype(F32), yr3)


def moe_block(x, xb, router_w, router_bias, w1, w3, w2, ln_g, ln_b, alpha):
    del xb
    T, D = x.shape
    eidx, gates = moe_router(x, router_w, router_bias)
    pos, blk_e, n_used, n_blocks = _dispatch_plan(eidx, MOE_ROWS)
    xr3 = moe_dispatch(x.reshape((T,) + _token_slab(D)), pos, n_blocks * MOE_ROWS)
    yr3 = moe_experts(xr3, blk_e, n_used, w1.astype(BF16), w3.astype(BF16), w2.astype(BF16), MOE_ROWS)
    return moe_combine(x, yr3, pos, gates.T, ln_g, ln_b, alpha)


def kernel(x_prompt, x_sample, a_w_qkv, a_w_o, b_w_in, b_lam_re, b_lam_im, b_log_dt, b_b_re, b_b_im, b_c_re,
           b_c_im, b_d, b_w_glu, c_w_in, c_lower_bounds, c_norm_g, c_w_o, router_w, router_bias, moe_w1, moe_w3,
           moe_w2, ln1_g, ln1_b, ln2_g, ln2_b):
    D = x_prompt.shape[-1]
    depth = ln1_g.shape[0]
    alpha = (2 * depth) ** 0.25
    seq_lens = (x_prompt.shape[1],) * x_prompt.shape[0] + (x_sample.shape[1],) * x_sample.shape[0]
    x = jnp.concatenate([x_prompt.reshape(-1, D), x_sample.reshape(-1, D)], 0).astype(F32)
    xb = x.astype(BF16)
    sm = jax.nn.softmax(c_lower_bounds.astype(F32), axis=0)
    lower_bounds = jnp.cumsum(sm, axis=0) - sm[0]
    for i in range(depth):
        j, kind = divmod(i, N_MIXERS)
        if kind == 0:
            x, xb = dilated_attention_layer(x, xb, a_w_qkv[j], a_w_o[j], ln1_g[i], ln1_b[i], alpha, seq_lens)
        else:
            if kind == 1:
                h = s5_mixer(xb, b_w_in[j], b_lam_re[j], b_lam_im[j], b_log_dt[j], b_b_re[j], b_b_im[j],
                             b_c_re[j], b_c_im[j], b_d[j], b_w_glu[j], seq_lens)
            else:
                h = hgrn2_mixer(xb, c_w_in[j], lower_bounds[i], c_norm_g[j], c_w_o[j], seq_lens)
            x, xb = residual_layer_norm(x, h, ln1_g[i], ln1_b[i], alpha)
        x, xb = moe_block(x, xb, router_w, router_bias, moe_w1[i], moe_w3[i], moe_w2[i], ln2_g[i], ln2_b[i], alpha)
    n_prompt = x_prompt.shape[0] * x_prompt.shape[1]
    return (x[:n_prompt].reshape(x_prompt.shape).astype(x_prompt.dtype),
            x[n_prompt:].reshape(x_sample.shape).astype(x_sample.dtype))
```

```python
import functools
import math

import jax
import jax.numpy as jnp
import numpy as np
from jax import lax
from jax.experimental import pallas as pl
from jax.experimental.pallas import tpu as pltpu

F32 = jnp.float32
BF16 = jnp.bfloat16

LANES = 128
SUBLANES = 8
VMEM_LIMIT = 56 * 1024 * 1024

A_WINDOWS = (128, 512, 2048)
A_DILATIONS = (1, 4, 16)
A_HEAD_DIM = 128
ROPE_THETA = 10000.0
S5_CH = 16
S5_STATE = 64
S5_MAX_RE = -1e-4
C_KDIM = 128
C_CHUNK = 32
N_EXPERTS = 32
N_EXPERT_GROUPS = 8
EXPERTS_PER_GROUP = N_EXPERTS // N_EXPERT_GROUPS
TOP_K = 2
LN_EPS = 1e-5
RMS_EPS = 1e-6
N_MIXERS = 3
NEG_BIG = -1e30


def _tile(n, pref, unit=LANES):
    if n <= pref:
        return n
    t = pref - pref % unit
    while n % t:
        t -= unit
    return t


def _cparams(n_axes):
    return pltpu.CompilerParams(dimension_semantics=("arbitrary",) * n_axes, vmem_limit_bytes=VMEM_LIMIT)


def _mm_body(x_ref, w_ref, o_ref):
    acc = jnp.dot(x_ref[...].astype(BF16), w_ref[...].astype(BF16), preferred_element_type=F32)
    o_ref[...] = acc.astype(o_ref.dtype)


def matmul(x, w, out_dtype, tm=1024, tn=1024):
    M, K = x.shape
    N = w.shape[1]
    tm, tn = _tile(M, tm, SUBLANES), _tile(N, tn)
    return pl.pallas_call(
        _mm_body,
        grid=(M // tm, N // tn),
        in_specs=[pl.BlockSpec((tm, K), lambda i, j: (i, 0)), pl.BlockSpec((K, tn), lambda i, j: (0, j))],
        out_specs=pl.BlockSpec((tm, tn), lambda i, j: (i, j)),
        out_shape=jax.ShapeDtypeStruct((M, N), out_dtype),
        compiler_params=_cparams(2),
        name="matmul",
    )(x, w)


def _mm_glu_body(x_ref, wv_ref, wg_ref, o_ref):
    x = x_ref[...].astype(BF16)
    val = jnp.dot(x, wv_ref[...].astype(BF16), preferred_element_type=F32)
    gate = jnp.dot(x, wg_ref[...].astype(BF16), preferred_element_type=F32)
    o_ref[...] = (val * jax.nn.sigmoid(gate)).astype(o_ref.dtype)


def matmul_glu(x, w, out_dtype, tm=1024, tn=512):
    M, K = x.shape
    N = w.shape[1] // 2
    tm, tn = _tile(M, tm, SUBLANES), _tile(N, tn)
    nj = N // tn
    return pl.pallas_call(
        _mm_glu_body,
        grid=(M // tm, nj),
        in_specs=[
            pl.BlockSpec((tm, K), lambda i, j: (i, 0)),
            pl.BlockSpec((K, tn), lambda i, j: (0, j)),
            pl.BlockSpec((K, tn), lambda i, j: (0, j + nj)),
        ],
        out_specs=pl.BlockSpec((tm, tn), lambda i, j: (i, j)),
        out_shape=jax.ShapeDtypeStruct((M, N), out_dtype),
        compiler_params=_cparams(2),
        name="matmul_glu",
    )(x, w, w)


def _layer_norm_rows(y, g, b):
    yc = y - jnp.mean(y, -1, keepdims=True)
    var = jnp.mean(yc * yc, -1, keepdims=True)
    return yc * lax.rsqrt(var + LN_EPS) * g + b


def _res_ln_body(x_ref, h_ref, g_ref, b_ref, o_ref, ob_ref, *, alpha):
    y = alpha * x_ref[...] + h_ref[...].astype(F32)
    out = _layer_norm_rows(y, g_ref[...], b_ref[...])
    o_ref[...] = out
    ob_ref[...] = out.astype(BF16)


def residual_layer_norm(x, h, g, b, alpha, tm=512):
    M, D = x.shape
    tm = _tile(M, tm, SUBLANES)
    row = pl.BlockSpec((tm, D), lambda i: (i, 0))
    vec = pl.BlockSpec((1, D), lambda i: (0, 0))
    return pl.pallas_call(
        functools.partial(_res_ln_body, alpha=alpha),
        grid=(M // tm,),
        in_specs=[row, row, vec, vec],
        out_specs=[row, row],
        out_shape=[jax.ShapeDtypeStruct((M, D), F32), jax.ShapeDtypeStruct((M, D), BF16)],
        compiler_params=_cparams(1),
        name="residual_layer_norm",
    )(x, h, g.reshape(1, D).astype(F32), b.reshape(1, D).astype(F32))


def _residue_major(n, dil):
    p = np.arange(n)
    src = (p % (n // dil)) * dil + p // (n // dil)
    return (src[:, None] == np.arange(n)[None, :]).astype(np.float32)


def _qkv_body(x_ref, w_ref, cos_ref, sin_ref, *rest, dil, n_rope_tiles):
    if dil > 1:
        perm_ref, o_ref, xp_ref = rest
    else:
        o_ref, xp_ref = rest
    j = pl.program_id(1)
    tm = x_ref.shape[0]

    @pl.when(j == 0)
    def _():
        if dil > 1:
            xp_ref[...] = jnp.dot(perm_ref[...], x_ref[...], preferred_element_type=F32).astype(BF16)
        else:
            xp_ref[...] = x_ref[...]

    acc = jnp.dot(xp_ref[...], w_ref[...], preferred_element_type=F32)
    rows = tm // dil

    @pl.when(j < n_rope_tiles)
    def _():
        cos = cos_ref[...].reshape(tm, A_HEAD_DIM)
        sin = sin_ref[...].reshape(tm, A_HEAD_DIM)
        for h in range(acc.shape[1] // A_HEAD_DIM):
            hs = slice(h * A_HEAD_DIM, (h + 1) * A_HEAD_DIM)
            t = acc[:, hs]
            t = (t * cos + pltpu.roll(t, A_HEAD_DIM // 2, 1) * sin).astype(o_ref.dtype)
            for r in range(dil):
                o_ref[r, :, hs] = t[r * rows:(r + 1) * rows]

    @pl.when(j >= n_rope_tiles)
    def _():
        for r in range(dil):
            o_ref[r] = acc[r * rows:(r + 1) * rows].astype(o_ref.dtype)


def project_qkv(xb, w, g, n_groups, cos, sin, dil, tm=1024, tn=1024):
    T, K = xb.shape
    W = w.shape[1] // (3 * n_groups)
    N = 3 * W
    tm = _tile(T, tm, SUBLANES * 2 * dil)
    tn = _tile(W, tn)
    wt = W // tn
    in_specs = [
        pl.BlockSpec((tm, K), lambda i, j: (i, 0)),
        pl.BlockSpec((K, tn), lambda i, j: (0, ((j // wt) * n_groups + g) * wt + j % wt)),
        pl.BlockSpec((dil, tm // dil, A_HEAD_DIM), lambda i, j: (0, i, 0)),
        pl.BlockSpec((dil, tm // dil, A_HEAD_DIM), lambda i, j: (0, i, 0)),
    ]
    args = [xb, w, cos, sin]
    if dil > 1:
        in_specs.append(pl.BlockSpec((tm, tm), lambda i, j: (0, 0)))
        args.append(jnp.asarray(_residue_major(tm, dil), BF16))
    return pl.pallas_call(
        functools.partial(_qkv_body, dil=dil, n_rope_tiles=(2 * N // 3) // tn),
        grid=(T // tm, N // tn),
        in_specs=in_specs,
        out_specs=pl.BlockSpec((dil, tm // dil, tn), lambda i, j: (0, i, j)),
        out_shape=jax.ShapeDtypeStruct((dil, T // dil, N), BF16),
        scratch_shapes=[pltpu.VMEM((tm, K), BF16)],
        compiler_params=_cparams(2),
        name=f"project_qkv_d{dil}",
    )(*args)


def _segment_bounds(seq_lens, dil, tile):
    lo, hi = [], []
    start = 0
    for _ in range(dil):
        for s in seq_lens:
            n = s // dil
            assert s % dil == 0 and n % tile == 0, (s, dil, tile)
            lo += [start] * (n // tile)
            hi += [start + n] * (n // tile)
            start += n
    return np.asarray(lo, np.int32), np.asarray(hi, np.int32)


def _attn_body(lo_ref, hi_ref, q_ref, k_ref, kp_ref, kn_ref, v_ref, vp_ref, vn_ref, o_ref, lse_ref,
               kbuf, vbuf, *, tq, nh, half, scale):
    i = pl.program_id(0)
    lo = lo_ref[i]
    hi = hi_ref[i]
    sb = 2 * half
    kbuf[0:half] = kp_ref[...]
    kbuf[half:half + tq] = k_ref[...]
    kbuf[half + tq:] = kn_ref[...]
    vbuf[0:half] = vp_ref[...]
    vbuf[half:half + tq] = v_ref[...]
    vbuf[half + tq:] = vn_ref[...]
    r = lax.broadcasted_iota(jnp.int32, (sb, 2 * sb), 0)
    c = lax.broadcasted_iota(jnp.int32, (sb, 2 * sb), 1)
    band = (c >= r) & (c <= r + 2 * half)
    lane = lax.broadcasted_iota(jnp.int32, (sb, LANES), 1)
    for j in range(tq // sb):
        kpos = i * tq + (j * sb - half) + c
        valid = band & (kpos >= lo) & (kpos < hi)
        lse_tile = jnp.zeros((sb, LANES), F32)
        for h in range(nh):
            hs = slice(h * A_HEAD_DIM, (h + 1) * A_HEAD_DIM)
            qj = q_ref[j * sb:(j + 1) * sb, hs]
            kj = kbuf[j * sb:(j + 2) * sb, hs]
            vj = vbuf[j * sb:(j + 2) * sb, hs]
            s = lax.dot_general(qj, kj, (((1,), (1,)), ((), ())), preferred_element_type=F32) * scale
            s = jnp.where(valid, s, NEG_BIG)
            m = jnp.max(s, -1, keepdims=True)
            p = jnp.exp(s - m)
            den = jnp.sum(p, -1, keepdims=True)
            o = jnp.dot(p.astype(BF16), vj, preferred_element_type=F32) / den
            o_ref[j * sb:(j + 1) * sb, hs] = o.astype(o_ref.dtype)
            lse_tile = jnp.where(lane == h, m + jnp.log(den), lse_tile)
        lse_ref[j * sb:(j + 1) * sb, :] = lse_tile


def dilated_attention_group(qkv, g, seq_lens, tq=256):
    T = qkv.shape[0]
    W = qkv.shape[1] // 3
    nh = W // A_HEAD_DIM
    dil = A_DILATIONS[g]
    half = A_WINDOWS[g] // (2 * dil)
    tq = min(tq, min(seq_lens) // max(A_DILATIONS))
    assert tq % (2 * half) == 0 and nh <= LANES
    n_tiles = T // tq
    hb = tq // half
    n_hblk = T // half
    lo, hi = _segment_bounds(seq_lens, dil, tq)
    prev = lambda i: jnp.maximum(i * hb - 1, 0)
    nxt = lambda i: jnp.minimum((i + 1) * hb, n_hblk - 1)
    in_specs = [
        pl.BlockSpec((tq, W), lambda i, lo, hi: (i, 0)),
        pl.BlockSpec((tq, W), lambda i, lo, hi: (i, 1)),
        pl.BlockSpec((half, W), lambda i, lo, hi: (prev(i), 1)),
        pl.BlockSpec((half, W), lambda i, lo, hi: (nxt(i), 1)),
        pl.BlockSpec((tq, W), lambda i, lo, hi: (i, 2)),
        pl.BlockSpec((half, W), lambda i, lo, hi: (prev(i), 2)),
        pl.BlockSpec((half, W), lambda i, lo, hi: (nxt(i), 2)),
    ]
    out_specs = [
        pl.BlockSpec((tq, W), lambda i, lo, hi: (i, 0)),
        pl.BlockSpec((tq, LANES), lambda i, lo, hi: (i, 0)),
    ]
    return pl.pallas_call(
        functools.partial(_attn_body, tq=tq, nh=nh, half=half, scale=A_HEAD_DIM ** -0.5),
        grid_spec=pltpu.PrefetchScalarGridSpec(
            num_scalar_prefetch=2,
            grid=(n_tiles,),
            in_specs=in_specs,
            out_specs=out_specs,
            scratch_shapes=[pltpu.VMEM((tq + 2 * half, W), BF16), pltpu.VMEM((tq + 2 * half, W), BF16)],
        ),
        out_shape=[jax.ShapeDtypeStruct((T, W), BF16), jax.ShapeDtypeStruct((T, LANES), F32)],
        compiler_params=_cparams(1),
        name=f"dilated_attention_g{g}",
    )(jnp.asarray(lo), jnp.asarray(hi), qkv, qkv, qkv, qkv, qkv, qkv, qkv)


def _attn_out_body(*refs, dils, nh, alpha):
    ng = len(dils)
    o_refs, l_refs = refs[:ng], refs[ng:2 * ng]
    pt_refs = refs[2 * ng:2 * ng + sum(d > 1 for d in dils)]
    wo_ref, x_ref, g_ref, b_ref, out_ref, outb_ref = refs[2 * ng + len(pt_refs):]
    tm = x_ref.shape[0]
    os_, ls = [], []
    k = 0
    for g, d in enumerate(dils):
        o = o_refs[g][...].reshape(tm, nh * A_HEAD_DIM)
        l = l_refs[g][...].reshape(tm, LANES)
        if d > 1:
            pt = pt_refs[k][...]
            k += 1
            o = jnp.dot(pt, o, preferred_element_type=F32)
            l = functools.reduce(jnp.add, [jnp.dot(pt, part, preferred_element_type=F32)
                                           for part in _split3_bf16(l)])
        os_.append(o)
        ls.append(l)
    m = functools.reduce(jnp.maximum, ls)
    es = [jnp.exp(l - m) for l in ls]
    den = functools.reduce(jnp.add, es)
    ws = [e / den for e in es]
    mixed = []
    for h in range(nh):
        hs = slice(h * A_HEAD_DIM, (h + 1) * A_HEAD_DIM)
        acc = ws[0][:, h:h + 1] * os_[0][:, hs]
        for g in range(1, ng):
            acc = acc + ws[g][:, h:h + 1] * os_[g][:, hs]
        mixed.append(acc.astype(BF16))
    hmix = jnp.dot(jnp.concatenate(mixed, axis=1), wo_ref[...], preferred_element_type=F32)
    out = _layer_norm_rows(alpha * x_ref[...] + hmix, g_ref[...], b_ref[...])
    out_ref[...] = out
    outb_ref[...] = out.astype(BF16)


def attention_output(x, outs, lses, w_o, ln_g, ln_b, alpha, tm=256):
    T, D = x.shape
    W = outs[0].shape[1]
    dils = A_DILATIONS
    tm = _tile(T, tm, SUBLANES * 2 * max(dils))
    in_specs, args = [], []
    for arrs, width in ((outs, W), (lses, LANES)):
        for a, d in zip(arrs, dils):
            in_specs.append(pl.BlockSpec((d, tm // d, width), lambda i: (0, i, 0)))
            args.append(a.reshape(d, T // d, width))
    for d in dils:
        if d > 1:
            in_specs.append(pl.BlockSpec((tm, tm), lambda i: (0, 0)))
            args.append(jnp.asarray(_residue_major(tm, d).T, BF16))
    row = pl.BlockSpec((tm, D), lambda i: (i, 0))
    vec = pl.BlockSpec((1, D), lambda i: (0, 0))
    in_specs += [pl.BlockSpec((W, D), lambda i: (0, 0)), row, vec, vec]
    args += [w_o, x, ln_g.reshape(1, D).astype(F32), ln_b.reshape(1, D).astype(F32)]
    return pl.pallas_call(
        functools.partial(_attn_out_body, dils=dils, nh=W // A_HEAD_DIM, alpha=alpha),
        grid=(T // tm,),
        in_specs=in_specs,
        out_specs=[row, row],
        out_shape=[jax.ShapeDtypeStruct((T, D), F32), jax.ShapeDtypeStruct((T, D), BF16)],
        compiler_params=_cparams(1),
        name="attention_output",
    )(*args)


def _rope_tables(seq_lens, dil):
    half = A_HEAD_DIM // 2
    inv_freq = ROPE_THETA ** (-jnp.arange(half, dtype=F32) / half)
    pos = jnp.concatenate([jnp.arange(s, dtype=F32) for s in seq_lens])
    ang = pos[:, None] * inv_freq[None, :]
    cos, sin = jnp.cos(ang), jnp.sin(ang)
    regroup = lambda t: t.reshape(-1, dil, A_HEAD_DIM).transpose(1, 0, 2)
    return regroup(jnp.concatenate([cos, cos], -1)), regroup(jnp.concatenate([-sin, sin], -1))


def dilated_attention_layer(x, xb, w_qkv, w_o, ln_g, ln_b, alpha, seq_lens):
    n_groups = len(A_WINDOWS)
    T, D = x.shape
    W = w_qkv.shape[1] // (3 * n_groups)
    wb = w_qkv.astype(BF16)
    outs, lses = [], []
    for g in range(n_groups):
        dil = A_DILATIONS[g]
        cos, sin = _rope_tables(seq_lens, dil)
        qkv = project_qkv(xb, wb, g, n_groups, cos, sin, dil)
        o, lse = dilated_attention_group(qkv.reshape(T, 3 * W), g, seq_lens)
        outs.append(o)
        lses.append(lse)
    return attention_output(x, outs, lses, w_o.astype(BF16), ln_g, ln_b, alpha)


S5_GROUP_BLOCK = LANES // S5_CH
S5_NS = S5_GROUP_BLOCK * S5_STATE
S5_STREAMS = SUBLANES
S5_BLOCKS_PER_STEP = 2


def _cmul(ar, ai, br, bi):
    return ar * br - ai * bi, ar * bi + ai * br


def _split3_bf16(x):
    hi = x.astype(BF16)
    r1 = x - hi.astype(F32)
    mid = r1.astype(BF16)
    lo = (r1 - mid.astype(F32)).astype(BF16)
    return hi, mid, lo


def _s5_body(reset_ref, u_ref, d_ref, bblk_ref, cblk_ref, lam_ref, p_ref, pt_ref, *rest, tt, rev, final):
    if final:
        yin_ref, o_ref, bu_ref, carry_ref = rest
    else:
        o_ref, bu_ref, carry_ref = rest
    i = pl.program_id(1)
    ns = S5_NS
    nblk = bblk_ref.shape[0]
    nc = nblk * (ns // LANES)
    lc = tt // S5_STREAMS
    ss = S5_STREAMS

    @pl.when(reset_ref[i] == 1)
    def _():
        carry_ref[...] = jnp.zeros_like(carry_ref)

    perm = p_ref[...]
    u = functools.reduce(jnp.add, [jnp.dot(perm, part, preferred_element_type=F32)
                                   for part in _split3_bf16(u_ref[...])])
    for s in range(nblk):
        bu_ref[:, s * 2 * ns:(s + 1) * 2 * ns] = jnp.dot(u[:, s * LANES:(s + 1) * LANES].astype(BF16), bblk_ref[s],
                                                         preferred_element_type=F32)
    order = range(lc - 1, -1, -1) if rev else range(lc)
    row = lax.broadcasted_iota(jnp.int32, (ss, LANES), 0)
    last = 0 if rev else ss - 1
    first = ss - 1 if rev else 0
    shift1 = ss - 1 if rev else 1
    cpb = ns // LANES
    re_cols = [slice((c // cpb) * 2 * ns + (c % cpb) * LANES, (c // cpb) * 2 * ns + (c % cpb + 1) * LANES)
               for c in range(nc)]
    im_cols = [slice(sl.start + ns, sl.stop + ns) for sl in re_cols]
    rows = lambda k, c: jnp.broadcast_to(lam_ref[c // cpb, k:k + 1, (c % cpb) * LANES:(c % cpb + 1) * LANES],
                                         (ss, LANES))
    lr = [rows(0, c) for c in range(nc)]
    li = [rows(1, c) for c in range(nc)]

    hr = [jnp.zeros((ss, LANES), F32) for _ in range(nc)]
    hi = [jnp.zeros((ss, LANES), F32) for _ in range(nc)]
    for j in order:
        js = slice(j * ss, (j + 1) * ss)
        for c in range(nc):
            pr, pi = _cmul(lr[c], li[c], hr[c], hi[c])
            hr[c], hi[c] = pr + bu_ref[js, re_cols[c]], pi + bu_ref[js, im_cols[c]]

    for c in range(nc):
        er, ei = hr[c], hi[c]
        cr = jnp.broadcast_to(carry_ref[last:last + 1, re_cols[c]], (ss, LANES))
        ci = jnp.broadcast_to(carry_ref[last:last + 1, im_cols[c]], (ss, LANES))
        sr = jnp.where(row == first, cr, pltpu.roll(er, shift1, 0))
        si = jnp.where(row == first, ci, pltpu.roll(ei, shift1, 0))
        for n, k in enumerate((1, 2, 4)):
            keep = (row <= ss - 1 - k) if rev else (row >= k)
            shift = ss - k if rev else k
            qr = jnp.where(keep, pltpu.roll(sr, shift, 0), 0.0)
            qi = jnp.where(keep, pltpu.roll(si, shift, 0), 0.0)
            pr, pi = _cmul(rows(2 + 2 * n, c), rows(3 + 2 * n, c), qr, qi)
            sr, si = sr + pr, si + pi
        pr, pi = _cmul(rows(2, c), rows(3, c), sr, si)
        carry_ref[:, re_cols[c]] = pr + er
        carry_ref[:, im_cols[c]] = pi + ei
        hr[c], hi[c] = sr, si

    for j in order:
        js = slice(j * ss, (j + 1) * ss)
        for c in range(nc):
            pr, pi = _cmul(lr[c], li[c], hr[c], hi[c])
            hr[c], hi[c] = pr + bu_ref[js, re_cols[c]], pi + bu_ref[js, im_cols[c]]
            bu_ref[js, re_cols[c]] = hr[c]
            bu_ref[js, im_cols[c]] = hi[c]

    y = jnp.concatenate([jnp.dot(bu_ref[:, s * 2 * ns:(s + 1) * 2 * ns].astype(BF16), cblk_ref[s],
                                 preferred_element_type=F32) for s in range(nblk)], axis=1)
    if final:
        y = jax.nn.gelu(yin_ref[...] + y).astype(BF16)
        o_ref[...] = jnp.dot(pt_ref[...], y, preferred_element_type=F32).astype(o_ref.dtype)
    else:
        o_ref[...] = u * d_ref[...] + y


def _s5_direction_params(lam_re, lam_im, log_dt, b_re, b_im, c_re, c_im, lc):
    G = lam_re.shape[0]
    nb = G // S5_GROUP_BLOCK
    a = jnp.minimum(lam_re.astype(F32), S5_MAX_RE)
    b = lam_im.astype(F32)
    dt = jnp.exp(log_dt.astype(F32))[:, None]

    def lam_pow(n):
        mag = jnp.exp(n * a * dt)
        return mag * jnp.cos(n * b * dt), mag * jnp.sin(n * b * dt)

    lbr, lbi = lam_pow(1.0)
    nr, ni = lbr - 1.0, lbi
    den = a * a + b * b
    fr, fi = (nr * a + ni * b) / den, (ni * a - nr * b) / den
    bbr = fr[..., None] * b_re - fi[..., None] * b_im
    bbi = fr[..., None] * b_im + fi[..., None] * b_re
    eye = jnp.eye(S5_GROUP_BLOCK, dtype=F32)

    def in_block(t):
        t = t.reshape(nb, S5_GROUP_BLOCK, S5_STATE, S5_CH)
        return jnp.einsum('ngpc,gh->ngchp', t, eye).reshape(nb, LANES, S5_NS)

    def out_block(t):
        t = t.reshape(nb, S5_GROUP_BLOCK, S5_CH, S5_STATE)
        return jnp.einsum('ngcp,gh->nhpgc', t, eye).reshape(nb, S5_NS, LANES)

    bblk = jnp.concatenate([in_block(bbr), in_block(bbi)], -1).astype(BF16)
    cblk = jnp.concatenate([out_block(c_re.astype(F32)), out_block(-c_im.astype(F32))], 1).astype(BF16)
    vecs = []
    for n in (1.0, float(lc), 2.0 * lc, 4.0 * lc):
        vecs += list(lam_pow(n))
    lam_rows = jnp.stack([v.reshape(nb, S5_NS) for v in vecs], 1)
    return bblk, cblk, lam_rows


def _s5_scan(u, d_skip, params, y_in, seq_lens, rev, tt):
    T, D = u.shape
    nb = D // LANES
    nt = T // tt
    bblk, cblk, lam_rows = params
    starts = np.cumsum([0] + list(seq_lens))[:-1] // tt
    ends = np.cumsum(list(seq_lens)) // tt - 1
    reset = np.zeros((nt,), np.int32)
    reset[(nt - 1 - ends) if rev else starts] = 1
    tmap = (lambda i: nt - 1 - i) if rev else (lambda i: i)
    final = y_in is not None
    nblk = S5_BLOCKS_PER_STEP if nb % S5_BLOCKS_PER_STEP == 0 else 1
    tile = pl.BlockSpec((tt, nblk * LANES), lambda b, i, rs: (tmap(i), b))
    in_specs = [
        tile,
        pl.BlockSpec((1, nblk * LANES), lambda b, i, rs: (0, b)),
        pl.BlockSpec((nblk, LANES, 2 * S5_NS), lambda b, i, rs: (b, 0, 0)),
        pl.BlockSpec((nblk, 2 * S5_NS, LANES), lambda b, i, rs: (b, 0, 0)),
        pl.BlockSpec((nblk, SUBLANES, S5_NS), lambda b, i, rs: (b, 0, 0)),
        pl.BlockSpec((tt, tt), lambda b, i, rs: (0, 0)),
        pl.BlockSpec((tt, tt), lambda b, i, rs: (0, 0)),
    ]
    src = (np.arange(tt) % S5_STREAMS) * (tt // S5_STREAMS) + np.arange(tt) // S5_STREAMS
    perm = (src[:, None] == np.arange(tt)[None, :]).astype(np.float32)
    args = [u, d_skip.reshape(1, D).astype(F32), bblk, cblk, lam_rows,
            jnp.asarray(perm, BF16), jnp.asarray(perm.T, BF16)]
    if final:
        in_specs.append(tile)
        args.append(y_in)
    return pl.pallas_call(
        functools.partial(_s5_body, tt=tt, rev=rev, final=final),
        grid_spec=pltpu.PrefetchScalarGridSpec(
            num_scalar_prefetch=1,
            grid=(nb // nblk, nt),
            in_specs=in_specs,
            out_specs=tile,
            scratch_shapes=[pltpu.VMEM((tt, nblk * 2 * S5_NS), F32),
                            pltpu.VMEM((S5_STREAMS, nblk * 2 * S5_NS), F32)],
        ),
        out_shape=jax.ShapeDtypeStruct((T, D), BF16 if final else F32),
        compiler_params=_cparams(2),
        name="s5_scan_rev" if rev else "s5_scan_fwd",
    )(jnp.asarray(reset), *args)


def s5_mixer(xb, w_in, lam_re, lam_im, log_dt, b_re, b_im, c_re, c_im, d_skip, w_glu, seq_lens, tt=512):
    tt = min(tt, min(seq_lens))
    assert all(s % tt == 0 for s in seq_lens) and tt % S5_STREAMS == 0
    u = matmul(xb, w_in.astype(BF16), F32)
    lc = tt // S5_STREAMS
    prm = [_s5_direction_params(lam_re[r], lam_im[r], log_dt[r], b_re[r], b_im[r], c_re[r], c_im[r], lc)
           for r in range(2)]
    y = _s5_scan(u, d_skip, prm[0], None, seq_lens, False, tt)
    y = _s5_scan(u, d_skip, prm[1], y, seq_lens, True, tt)
    return matmul_glu(y, w_glu.astype(BF16), F32)


def _hgrn_body(reset_ref, zq_ref, zf_ref, zi_ref, lb_ref, *rest, tt, nh, rev, final):
    if final:
        zg_ref, ofwd_ref, ng_ref, o_ref, st_ref = rest
    else:
        o_ref, st_ref = rest
    i = pl.program_id(1)
    C = C_CHUNK

    @pl.when(reset_ref[i] == 1)
    def _():
        st_ref[...] = jnp.zeros_like(st_ref)

    r = lax.broadcasted_iota(jnp.int32, (tt, tt), 0)
    c = lax.broadcasted_iota(jnp.int32, (tt, tt), 1)
    same = (r // C) == (c // C)
    cum_m = same & ((c >= r) if rev else (c <= r))
    sum_m = jnp.concatenate([cum_m, same], 0).astype(BF16)
    lb = lb_ref[...]
    f = lb + (1.0 - lb) * jax.nn.sigmoid(zf_ref[...])
    k = 1.0 - f
    sums = functools.reduce(jnp.add, [jnp.dot(sum_m, part, preferred_element_type=F32)
                                      for part in _split3_bf16(jnp.log(f))])
    bcum, tot = sums[:tt], sums[tt:]
    qd = (jax.nn.silu(zq_ref[...]) * jnp.exp(bcum)).astype(BF16)
    ki = (k * jnp.exp(-bcum)).astype(BF16)
    ke = (k * jnp.exp(tot - bcum)).astype(BF16)
    v = zi_ref[...].astype(BF16)
    dec = jnp.exp(tot)
    rr = lax.broadcasted_iota(jnp.int32, (C, C), 0)
    cc = lax.broadcasted_iota(jnp.int32, (C, C), 1)
    tri = (cc >= rr) if rev else (cc <= rr)
    nt_dims = (((1,), (1,)), ((), ()))
    chunks = range(tt // C - 1, -1, -1) if rev else range(tt // C)
    for n in chunks:
        rs = slice(n * C, (n + 1) * C)
        for h in range(nh):
            hs = slice(h * C_KDIM, (h + 1) * C_KDIM)
            qh, vh = qd[rs, hs], v[rs, hs]
            att = lax.dot_general(qh, ki[rs, hs], nt_dims, preferred_element_type=F32)
            att = jnp.where(tri, att, 0.0).astype(BF16)
            st = st_ref[h]
            o = (jnp.dot(att, vh, preferred_element_type=F32)
                 + lax.dot_general(qh, st.astype(BF16), nt_dims, preferred_element_type=F32))
            st_ref[h] = st * dec[n * C:n * C + 1, hs] + jnp.dot(vh.T, ke[rs, hs], preferred_element_type=F32)
            if final:
                o = o + ofwd_ref[rs, hs]
                o = o * lax.rsqrt(jnp.mean(o * o, -1, keepdims=True) + RMS_EPS) * ng_ref[...]
                o = o * jax.nn.silu(zg_ref[rs, hs])
            o_ref[rs, hs] = o.astype(o_ref.dtype)


def _hgrn_scan(z, lb, seq_lens, rev, tt, nh, o_fwd=None, norm_g=None):
    T = z.shape[0]
    D = z.shape[1] // 5
    W = nh * C_KDIM
    nhb = D // W
    nt = T // tt
    starts = np.cumsum([0] + list(seq_lens))[:-1] // tt
    ends = np.cumsum(list(seq_lens)) // tt - 1
    reset = np.zeros((nt,), np.int32)
    reset[(nt - 1 - ends) if rev else starts] = 1
    tmap = (lambda i: nt - 1 - i) if rev else (lambda i: i)
    final = o_fwd is not None
    zcol = lambda n: pl.BlockSpec((tt, W), lambda b, i, rs: (tmap(i), n * nhb + b))
    in_specs = [zcol(0), zcol(2 if rev else 1), zcol(3), pl.BlockSpec((1, W), lambda b, i, rs: (0, b))]
    args = [z, z, z, lb.reshape(1, D).astype(F32)]
    if final:
        in_specs += [zcol(4), pl.BlockSpec((tt, W), lambda b, i, rs: (tmap(i), b)),
                     pl.BlockSpec((1, C_KDIM), lambda b, i, rs: (0, 0))]
        args += [z, o_fwd, norm_g.reshape(1, C_KDIM).astype(F32)]
    return pl.pallas_call(
        functools.partial(_hgrn_body, tt=tt, nh=nh, rev=rev, final=final),
        grid_spec=pltpu.PrefetchScalarGridSpec(
            num_scalar_prefetch=1,
            grid=(nhb, nt),
            in_specs=in_specs,
            out_specs=pl.BlockSpec((tt, W), lambda b, i, rs: (tmap(i), b)),
            scratch_shapes=[pltpu.VMEM((nh, C_KDIM, C_KDIM), F32)],
        ),
        out_shape=jax.ShapeDtypeStruct((T, D), BF16 if final else F32),
        compiler_params=_cparams(2),
        name="hgrn2_scan_rev" if rev else "hgrn2_scan_fwd",
    )(jnp.asarray(reset), *args)


def hgrn2_mixer(xb, w_in, lb, norm_g, w_o, seq_lens, tt=128, nh=16):
    assert all(s % tt == 0 for s in seq_lens) and tt % C_CHUNK == 0
    nh = min(nh, xb.shape[1] // C_KDIM)
    z = matmul(xb, w_in.astype(BF16), F32)
    o_fwd = _hgrn_scan(z, lb, seq_lens, False, tt, nh)
    o = _hgrn_scan(z, lb, seq_lens, True, tt, nh, o_fwd, norm_g)
    return matmul(o, w_o.astype(BF16), F32)


MOE_ROWS = 256


def _token_slab(d):
    return (d // LANES, LANES)


def _router_body(x_ref, wt_ref, bias_ref, eidx_ref, gate_ref):
    x = x_ref[...]
    wt = wt_ref[...]
    xh = x.astype(BF16)
    xl = (x - xh.astype(F32)).astype(BF16)
    wh = wt.astype(BF16)
    wl = (wt - wh.astype(F32)).astype(BF16)
    nt_dims = (((1,), (1,)), ((), ()))
    dg = lambda a, b: lax.dot_general(a, b, nt_dims, preferred_element_type=F32)
    scores = jax.nn.sigmoid(dg(wh, xh) + dg(wh, xl) + dg(wl, xh))
    sel = scores + bias_ref[...]
    ng = N_EXPERT_GROUPS
    s = [sel[j * ng:(j + 1) * ng] for j in range(EXPERTS_PER_GROUP)]
    sc = [scores[j * ng:(j + 1) * ng] for j in range(EXPERTS_PER_GROUP)]
    hi1, lo1 = jnp.maximum(s[0], s[1]), jnp.minimum(s[0], s[1])
    hi2, lo2 = jnp.maximum(s[2], s[3]), jnp.minimum(s[2], s[3])
    gs = jnp.maximum(hi1, hi2) + jnp.maximum(jnp.minimum(hi1, hi2), jnp.maximum(lo1, lo2))
    gi = lax.broadcasted_iota(jnp.int32, gs.shape, 0)
    gidx = jnp.min(jnp.where(gs == jnp.max(gs, 0, keepdims=True), gi, ng), 0, keepdims=True)
    gm = gi == gidx
    val = [jnp.sum(jnp.where(gm, t, 0.0), 0, keepdims=True) for t in s]
    scv = [jnp.sum(jnp.where(gm, t, 0.0), 0, keepdims=True) for t in sc]
    best, bj = val[0], jnp.zeros_like(gidx)
    for j in range(1, EXPERTS_PER_GROUP):
        upd = val[j] > best
        best, bj = jnp.where(upd, val[j], best), jnp.where(upd, j, bj)
    sec, sj = jnp.full_like(best, -jnp.inf), jnp.zeros_like(gidx)
    for j in range(EXPERTS_PER_GROUP):
        upd = (bj != j) & (val[j] > sec)
        sec, sj = jnp.where(upd, val[j], sec), jnp.where(upd, j, sj)
    pick = lambda idx: functools.reduce(jnp.add, [jnp.where(idx == j, scv[j], 0.0)
                                                  for j in range(EXPERTS_PER_GROUP)])
    g1, g2 = pick(bj), pick(sj)
    eidx_ref[0:1, :] = gidx * EXPERTS_PER_GROUP + bj
    eidx_ref[1:2, :] = gidx * EXPERTS_PER_GROUP + sj
    gate_ref[0:1, :] = g1 / (g1 + g2)
    gate_ref[1:2, :] = g2 / (g1 + g2)


def moe_router(x, router_w, router_bias, tm=512):
    T, D = x.shape
    tm = _tile(T, tm)
    perm = np.arange(N_EXPERTS).reshape(N_EXPERT_GROUPS, EXPERTS_PER_GROUP).T.reshape(-1)
    wt = router_w.astype(F32).T[perm]
    bias = router_bias.astype(F32)[perm].reshape(N_EXPERTS, 1)
    return pl.pallas_call(
        _router_body,
        grid=(T // tm,),
        in_specs=[pl.BlockSpec((tm, D), lambda i: (i, 0)), pl.BlockSpec((N_EXPERTS, D), lambda i: (0, 0)),
                  pl.BlockSpec((N_EXPERTS, 1), lambda i: (0, 0))],
        out_specs=[pl.BlockSpec((TOP_K, tm), lambda i: (0, i)), pl.BlockSpec((TOP_K, tm), lambda i: (0, i))],
        out_shape=[jax.ShapeDtypeStruct((TOP_K, T), jnp.int32), jax.ShapeDtypeStruct((TOP_K, T), F32)],
        compiler_params=_cparams(1),
        name="moe_router",
    )(x, wt, bias)


def _dispatch_plan(eidx, blk):
    T = eidx.shape[1]
    A = T * TOP_K
    flat_e = eidx.T.reshape(A)
    onehot = (flat_e[:, None] == jnp.arange(N_EXPERTS, dtype=jnp.int32)[None, :]).astype(jnp.int32)
    ck = _tile(A, 256, SUBLANES)
    within = jnp.einsum('ts,nse->nte', jnp.tril(jnp.ones((ck, ck), BF16)),
                        onehot.astype(BF16).reshape(A // ck, ck, N_EXPERTS), preferred_element_type=F32)
    chunk_tot = within[:, -1, :]
    csum = (within + (jnp.cumsum(chunk_tot, axis=0) - chunk_tot)[:, None, :]).reshape(A, N_EXPERTS).astype(jnp.int32)
    counts = csum[-1]
    padded = (counts + blk - 1) // blk * blk
    pend = jnp.cumsum(padded)
    pstart = pend - padded
    pos = jnp.sum(onehot * (csum - 1 + pstart[None, :]), axis=1).astype(jnp.int32)
    n_blocks = A // blk + N_EXPERTS
    first_row = jnp.arange(n_blocks, dtype=jnp.int32) * blk
    blk_e = jnp.minimum(jnp.sum((pend[None, :] <= first_row[:, None]).astype(jnp.int32), axis=1), N_EXPERTS - 1)
    n_used = (pend[-1] // blk).astype(jnp.int32).reshape(1)
    return pos, blk_e, n_used, n_blocks


def _dispatch_body(pos_ref, x_ref, xr_in_ref, xr_ref, sem):
    del xr_in_ref
    tm = x_ref.shape[0]

    def copy(t, a):
        return pltpu.make_async_copy(x_ref.at[t], xr_ref.at[pos_ref[0, a]], sem)

    def issue(t, carry):
        for k in range(TOP_K):
            copy(t, TOP_K * t + k).start()
        return carry

    lax.fori_loop(0, tm, issue, 0)
    for k in range(TOP_K):
        pltpu.make_async_copy(x_ref, xr_ref.at[pl.ds(0, tm)], sem).wait()


def moe_dispatch(x3, pos, rows, tm=256):
    T = x3.shape[0]
    tm = _tile(T, tm, SUBLANES)
    pos3 = pos.reshape(T // tm, 1, TOP_K * tm)
    return pl.pallas_call(
        _dispatch_body,
        grid=(T // tm,),
        in_specs=[pl.BlockSpec((None, 1, TOP_K * tm), lambda i: (i, 0, 0), memory_space=pltpu.SMEM),
                  pl.BlockSpec((tm,) + x3.shape[1:], lambda i: (i, 0, 0)),
                  pl.BlockSpec(memory_space=pl.ANY)],
        out_specs=pl.BlockSpec(memory_space=pl.ANY),
        out_shape=jax.ShapeDtypeStruct((rows,) + x3.shape[1:], x3.dtype),
        scratch_shapes=[pltpu.SemaphoreType.DMA(())],
        input_output_aliases={2: 0},
        compiler_params=_cparams(1),
        name="moe_dispatch",
    )(pos3, x3, jnp.zeros((rows,) + x3.shape[1:], x3.dtype))


def _slab_to_rows(ref):
    return jnp.concatenate([ref[:, j, :] for j in range(ref.shape[1])], axis=1)


def _expert_body(be_ref, nu_ref, x_ref, w1_ref, w3_ref, w2_ref, o_ref):
    del be_ref

    @pl.when(pl.program_id(0) < nu_ref[0])
    def _():
        x = _slab_to_rows(x_ref).astype(BF16)
        h1 = jnp.dot(x, w1_ref[0], preferred_element_type=F32)
        h3 = jnp.dot(x, w3_ref[0], preferred_element_type=F32)
        y = jnp.dot((jax.nn.silu(h1) * h3).astype(BF16), w2_ref[0], preferred_element_type=F32)
        w = o_ref.shape[2]
        for j in range(o_ref.shape[1]):
            o_ref[:, j, :] = y[:, j * w:(j + 1) * w]

    @pl.when(pl.program_id(0) >= nu_ref[0])
    def _():
        o_ref[...] = jnp.zeros_like(o_ref)


def moe_experts(xr3, blk_e, n_used, w1, w3, w2, blk):
    rows = xr3.shape[0]
    E, D, F = w1.shape
    slab = pl.BlockSpec((blk,) + xr3.shape[1:], lambda b, be, nu: (b, 0, 0))
    return pl.pallas_call(
        _expert_body,
        grid_spec=pltpu.PrefetchScalarGridSpec(
            num_scalar_prefetch=2,
            grid=(rows // blk,),
            in_specs=[slab,
                      pl.BlockSpec((1, D, F), lambda b, be, nu: (be[b], 0, 0)),
                      pl.BlockSpec((1, D, F), lambda b, be, nu: (be[b], 0, 0)),
                      pl.BlockSpec((1, F, D), lambda b, be, nu: (be[b], 0, 0))],
            out_specs=slab,
        ),
        out_shape=jax.ShapeDtypeStruct(xr3.shape, F32),
        compiler_params=_cparams(1),
        name="moe_experts",
    )(blk_e, n_used, xr3, w1, w3, w2)


def _combine_body(pos_ref, gate_ref, x_ref, g_ref, b_ref, yr_ref, o_ref, ob_ref, buf0, buf1, sem, *, alpha):
    tm = x_ref.shape[0]
    bufs = (buf0, buf1)

    def copy(t, k):
        return pltpu.make_async_copy(yr_ref.at[pos_ref[0, TOP_K * t + k]], bufs[k].at[t], sem)

    def issue(t, carry):
        for k in range(TOP_K):
            copy(t, k).start()
        return carry

    lax.fori_loop(0, tm, issue, 0)
    for k in range(TOP_K):
        pltpu.make_async_copy(yr_ref.at[pl.ds(0, tm)], bufs[k], sem).wait()
    gates = gate_ref[...]
    f = gates[:, 0:1] * _slab_to_rows(buf0) + gates[:, 1:2] * _slab_to_rows(buf1)
    out = _layer_norm_rows(alpha * x_ref[...] + f, g_ref[...], b_ref[...])
    o_ref[...] = out
    ob_ref[...] = out.astype(BF16)


def moe_combine(x, yr3, pos, gates, g, b, alpha, tm=256):
    T, D = x.shape
    tm = _tile(T, tm, SUBLANES)
    pos3 = pos.reshape(T // tm, 1, TOP_K * tm)
    row = pl.BlockSpec((tm, D), lambda i: (i, 0))
    vec = pl.BlockSpec((1, D), lambda i: (0, 0))
    return pl.pallas_call(
        functools.partial(_combine_body, alpha=alpha),
        grid=(T // tm,),
        in_specs=[pl.BlockSpec((None, 1, TOP_K * tm), lambda i: (i, 0, 0), memory_space=pltpu.SMEM),
                  pl.BlockSpec((tm, TOP_K), lambda i: (i, 0)), row, vec, vec,
                  pl.BlockSpec(memory_space=pl.ANY)],
        out_specs=[row, row],
        out_shape=[jax.ShapeDtypeStruct((T, D), F32), jax.ShapeDtypeStruct((T, D), BF16)],
        scratch_shapes=[pltpu.VMEM((tm,) + yr3.shape[1:], F32), pltpu.VMEM((tm,) + yr3.shape[1:], F32),
                        pltpu.SemaphoreType.DMA(())],
        compiler_params=_cparams(1),
        name="moe_combine",
    )(pos3, gates, x, g.reshape(1, D).astype(F32), b.reshape(1, D).astype(F32), yr3)


def moe_block(x, xb, router_w, router_bias, w1, w3, w2, ln_g, ln_b, alpha):
    del xb
    T, D = x.shape
    eidx, gates = moe_router(x, router_w, router_bias)
    pos, blk_e, n_used, n_blocks = _dispatch_plan(eidx, MOE_ROWS)
    xr3 = moe_dispatch(x.reshape((T,) + _token_slab(D)), pos, n_blocks * MOE_ROWS)
    yr3 = moe_experts(xr3, blk_e, n_used, w1.astype(BF16), w3.astype(BF16), w2.astype(BF16), MOE_ROWS)
    return moe_combine(x, yr3, pos, gates.T, ln_g, ln_b, alpha)


def kernel(x_prompt, x_sample, a_w_qkv, a_w_o, b_w_in, b_lam_re, b_lam_im, b_log_dt, b_b_re, b_b_im, b_c_re,
           b_c_im, b_d, b_w_glu, c_w_in, c_lower_bounds, c_norm_g, c_w_o, router_w, router_bias, moe_w1, moe_w3,
           moe_w2, ln1_g, ln1_b, ln2_g, ln2_b):
    D = x_prompt.shape[-1]
    depth = ln1_g.shape[0]
    alpha = (2 * depth) ** 0.25
    seq_lens = (x_prompt.shape[1],) * x_prompt.shape[0] + (x_sample.shape[1],) * x_sample.shape[0]
    x = jnp.concatenate([x_prompt.reshape(-1, D), x_sample.reshape(-1, D)], 0).astype(F32)
    xb = x.astype(BF16)
    sm = jax.nn.softmax(c_lower_bounds.astype(F32), axis=0)
    lower_bounds = jnp.cumsum(sm, axis=0) - sm[0]
    for i in range(depth):
        j, kind = divmod(i, N_MIXERS)
        if kind == 0:
            x, xb = dilated_attention_layer(x, xb, a_w_qkv[j], a_w_o[j], ln1_g[i], ln1_b[i], alpha, seq_lens)
        else:
            if kind == 1:
                h = s5_mixer(xb, b_w_in[j], b_lam_re[j], b_lam_im[j], b_log_dt[j], b_b_re[j], b_b_im[j],
                             b_c_re[j], b_c_im[j], b_d[j], b_w_glu[j], seq_lens)
            else:
                h = hgrn2_mixer(xb, c_w_in[j], lower_bounds[i], c_norm_g[j], c_w_o[j], seq_lens)
            x, xb = residual_layer_norm(x, h, ln1_g[i], ln1_b[i], alpha)
        x, xb = moe_block(x, xb, router_w, router_bias, moe_w1[i], moe_w3[i], moe_w2[i], ln2_g[i], ln2_b[i], alpha)
    n_prompt = x_prompt.shape[0] * x_prompt.shape[1]
    return (x[:n_prompt].reshape(x_prompt.shape).astype(x_prompt.dtype),
            x[n_prompt:].reshape(x_sample.shape).astype(x_sample.dtype))
```
